```python
import jax
import jax.numpy as jnp
from jax import lax
import numpy as np

D_MODEL = 1024
BATCH = 4
SEQ = 4096
DEPTH = 4
DEC_BATCH = 128
DEC_SEQ = 1
PAST_LEN = 2048
PAGE_SIZE = 128

D_HEAD = 64
N_HEADS = D_MODEL // D_HEAD
H_A = N_HEADS // 4
H_B = N_HEADS // 4
H_C = N_HEADS - H_A - H_B
W_A = H_A * D_HEAD
W_B = H_B * D_HEAD
W_C = H_C * D_HEAD
D_MIX = W_A + W_B + W_C
RWKV_DECAY_RANK = 32
RWKV_ICL_RANK = 32
RWKV_GATE_RANK = 64
RWKV_COLS = 3 * W_A + RWKV_DECAY_RANK + RWKV_ICL_RANK + RWKV_GATE_RANK
RWKV_LN_EPS = 64e-5
GDN_CONV = 4
GDN_QKV = 3 * W_B
GDN_COLS = GDN_QKV + W_B + 2 * H_B
GDN_CHUNK = 64
SB_COLS = 3 * W_C
SB_BLOCK = 128
QK_SCALE = D_HEAD ** -0.5
IN_COLS = RWKV_COLS + GDN_COLS + SB_COLS
D_FF = 2816
FFN_CONV = 3
N_MOD = 6
NORM_EPS = 1e-6

kernel_name = 'hybrid_rwkv7_gdn_stickbreak_decoder_step'


def f32(t):
    return t.astype(jnp.float32)


def split_cols(x, sizes):
    offsets = [int(o) for o in np.cumsum(sizes)[:-1]]
    return jnp.split(x, offsets, axis=-1)


def heads(x, n_heads):
    return x.reshape(x.shape[:-1] + (n_heads, D_HEAD))


def merge_heads(x):
    return x.reshape(x.shape[:-2] + (x.shape[-2] * x.shape[-1],))


def rmsnorm(x, g, eps=NORM_EPS):
    xf = f32(x)
    y = xf * lax.rsqrt(jnp.mean(xf * xf, axis=-1, keepdims=True) + eps)
    return y.astype(x.dtype) * g


def l2norm(x, eps=1e-6):
    xf = f32(x)
    return (xf * lax.rsqrt(jnp.sum(xf * xf, axis=-1, keepdims=True) + eps)).astype(x.dtype)


def causal_dwconv(x, buf, w):
    width = w.shape[0]
    t = x.shape[1]
    xp = jnp.concatenate([buf.astype(x.dtype), x], axis=1)
    y = xp[:, 0:t] * w[0]
    for i in range(1, width):
        y = y + xp[:, i:i + t] * w[i]
    return y, xp[:, t:]


def rwkv7_scan(r, logw, k, v, kk, a, s0):
    def step(s, xs):
        r_t, lw_t, k_t, v_t, kk_t, a_t = xs
        sa = jnp.einsum('bhk,bhkv->bhv', kk_t, s)
        s = (jnp.exp(lw_t)[..., None] * s - (kk_t * a_t)[..., None] * sa[..., None, :]
             + k_t[..., None] * v_t[..., None, :])
        return s, jnp.einsum('bhk,bhkv->bhv', r_t, s)
    xs = tuple(jnp.swapaxes(t, 0, 1) for t in (r, logw, k, v, kk, a))
    s, y = lax.scan(step, s0, xs)
    return jnp.swapaxes(y, 0, 1), s


def rwkv7_branch(pa, shift_prev, s0, lp):
    prev = jnp.concatenate([shift_prev[:, None, :].astype(pa.dtype), pa[:, :-1]], axis=1)
    xm = pa + lp['rwkv_mu'] * (prev - pa)
    r, k, v, wd, ad, gd = split_cols(xm, [W_A, W_A, W_A, RWKV_DECAY_RANK, RWKV_ICL_RANK, RWKV_GATE_RANK])
    w_raw = f32(lp['rwkv_w0'] + jnp.tanh(wd) @ lp['rwkv_w_up'])
    logw = -jnp.exp(-jax.nn.softplus(-w_raw) - 0.5)
    a = jax.nn.sigmoid(lp['rwkv_a0'] + ad @ lp['rwkv_a_up'])
    g = jax.nn.sigmoid(gd) @ lp['rwkv_g_up']
    kk = l2norm(heads(k * lp['rwkv_k_k'], H_A))
    k = k * (1 + (a - 1) * lp['rwkv_k_a'])
    rh, kh, vh = heads(r, H_A), heads(k, H_A), heads(v, H_A)
    y, s_new = rwkv7_scan(f32(rh), heads(logw, H_A), f32(kh), f32(vh), f32(kk), f32(heads(a, H_A)), f32(s0))
    mean = jnp.mean(y, axis=-1, keepdims=True)
    var = jnp.mean(jnp.square(y - mean), axis=-1, keepdims=True)
    yn = merge_heads((y - mean) * lax.rsqrt(var + RWKV_LN_EPS)) * lp['rwkv_ln_w'] + lp['rwkv_ln_b']
    bonus = merge_heads(jnp.sum(f32(rh * kh * lp['rwkv_r_k']), axis=-1, keepdims=True) * f32(vh))
    out = ((yn + bonus) * g).astype(pa.dtype)
    return out, pa[:, -1], s_new


def gdn_recurrent(q, k, v, glog, beta, s0):
    def step(s, xs):
        q_t, k_t, v_t, g_t, b_t = xs
        s = s * jnp.exp(g_t)[..., None, None]
        v_t = (v_t - jnp.einsum('bhk,bhkv->bhv', k_t, s)) * b_t[..., None]
        s = s + k_t[..., :, None] * v_t[..., None, :]
        return s, jnp.einsum('bhk,bhkv->bhv', q_t, s)
    xs = tuple(jnp.swapaxes(t, 0, 1) for t in (q, k, v, glog, beta))
    s, o = lax.scan(step, s0, xs)
    return jnp.swapaxes(o, 0, 1), s


def gdn_chunked(q, k, v, glog, beta, s0):
    B, T, H, D = q.shape
    n = T // GDN_CHUNK

    def blocks(x):
        x = x.reshape((B, n, GDN_CHUNK) + x.shape[2:])
        return jnp.moveaxis(jnp.moveaxis(x, 1, 0), 3, 2)

    qc, kc, vc, bc = blocks(q), blocks(k), blocks(v), blocks(beta)
    gc = jnp.cumsum(blocks(glog), axis=-1)
    kb = kc * bc[..., None]
    vb = vc * bc[..., None]
    idx = jnp.arange(GDN_CHUNK)
    incl = idx[:, None] >= idx[None, :]
    strict = idx[:, None] > idx[None, :]
    diff = gc[..., :, None] - gc[..., None, :]
    decay = jnp.where(incl, jnp.exp(jnp.where(incl, diff, 0.0)), 0.0)
    low = jnp.where(strict, jnp.einsum('nbhid,nbhjd->nbhij', kb, kc) * decay, 0.0)
    eye = jnp.eye(GDN_CHUNK, dtype=q.dtype)
    tmat = lax.linalg.triangular_solve(low + eye, jnp.broadcast_to(eye, low.shape), left_side=True, lower=True)
    u = tmat @ vb
    w = tmat @ (kb * jnp.exp(gc)[..., None])
    qk = jnp.where(incl, jnp.einsum('nbhid,nbhjd->nbhij', qc, kc) * decay, 0.0)

    def step(s, xs):
        q_i, k_i, u_i, w_i, g_i, qk_i = xs
        v_new = u_i - w_i @ s
        o = (q_i * jnp.exp(g_i)[..., None]) @ s + qk_i @ v_new
        g_last = g_i[..., -1:]
        s = s * jnp.exp(g_last)[..., None] + jnp.einsum('bhck,bhcv->bhkv', k_i * jnp.exp(g_last - g_i)[..., None], v_new)
        return s, o

    s, o = lax.scan(step, s0, (qc, kc, u, w, gc, qk))
    o = jnp.moveaxis(jnp.moveaxis(o, 2, 3), 0, 1).reshape(B, T, H, D)
    return o, s


def gdn_branch(pb, conv_prev, s0, lp, chunked):
    qkv, z, aa, bb = split_cols(pb, [GDN_QKV, W_B, H_B, H_B])
    qkv, conv_new = causal_dwconv(qkv, conv_prev, lp['gdn_conv_w'])
    q, k, v = split_cols(jax.nn.silu(qkv), [W_B, W_B, W_B])
    q = f32(l2norm(heads(q, H_B))) * QK_SCALE
    k = f32(l2norm(heads(k, H_B)))
    v = f32(heads(v, H_B))
    beta = jax.nn.sigmoid(f32(bb))
    glog = -jnp.exp(f32(lp['gdn_a_log'])) * jax.nn.softplus(f32(aa) + f32(lp['gdn_dt_bias']))
    if chunked:
        o, s_new = gdn_chunked(q, k, v, glog, beta, f32(s0))
    else:
        o, s_new = gdn_recurrent(q, k, v, glog, beta, f32(s0))
    o = rmsnorm(o, f32(lp['gdn_norm_g'])) * jax.nn.silu(f32(heads(z, H_B)))
    return merge_heads(o).astype(pb.dtype), conv_new, s_new


def stick_breaking_block(q_blk, q_pos, k, v, bias):
    z = f32(jnp.einsum('bqhd,bshd->bhqs', q_blk, k)) * QK_SCALE + f32(bias)[None, :, None, None]
    s_pos = jnp.arange(k.shape[1], dtype=jnp.int32)
    reach = s_pos[None, :] < q_pos[:, None]
    log_rest = jnp.where(reach, jax.nn.log_sigmoid(-z), 0.0)
    log_rest = lax.cumsum(log_rest, axis=3, reverse=True) - log_rest
    weight = jnp.where(reach, jnp.exp(jax.nn.log_sigmoid(z) + log_rest), 0.0)
    return jnp.einsum('bhqs,bshd->bqhd', weight.astype(v.dtype), v)


def stick_breaking_prompt(q, k, v, bias):
    B, T, H, D = q.shape
    nb = T // SB_BLOCK
    q_blocks = jnp.moveaxis(q.reshape(B, nb, SB_BLOCK, H, D), 1, 0)
    pos_blocks = jnp.arange(T, dtype=jnp.int32).reshape(nb, SB_BLOCK)
    out = lax.map(lambda qp: stick_breaking_block(qp[0], qp[1], k, v, bias), (q_blocks, pos_blocks))
    return jnp.moveaxis(out, 0, 1).reshape(B, T, H, D)


def decoder_layer(x, c, lp, st, kv_past):
    shift_prev, s_rwkv, s_gdn, gdn_conv_prev, ffn_conv_prev = st
    mod = jax.nn.silu(c) @ lp['w_ada'] + lp['b_ada']
    sh1, sc1, gt1, sh2, sc2, gt2 = jnp.split(mod[:, None, :], N_MOD, axis=-1)
    h = rmsnorm(x, lp['norm_g'][0]) * (1 + sc1) + sh1
    pa, pb, pc = split_cols(h @ lp['w_in'], [RWKV_COLS, GDN_COLS, SB_COLS])
    ya, shift_new, s_rwkv_new = rwkv7_branch(pa, shift_prev, s_rwkv, lp)
    yb, gdn_conv_new, s_gdn_new = gdn_branch(pb, gdn_conv_prev, s_gdn, lp, kv_past is None)
    q, k, v = split_cols(pc, [W_C, W_C, W_C])
    q, k, v = heads(q, H_C), heads(k, H_C), heads(v, H_C)
    if kv_past is None:
        yc = stick_breaking_prompt(q, k, v, lp['sb_bias'])
    else:
        k_past, v_past = kv_past
        k_all = jnp.concatenate([k_past.astype(k.dtype), k], axis=1)
        v_all = jnp.concatenate([v_past.astype(v.dtype), v], axis=1)
        q_pos = k_past.shape[1] + jnp.arange(q.shape[1], dtype=jnp.int32)
        yc = stick_breaking_block(q, q_pos, k_all, v_all, lp['sb_bias'])
    yc = merge_heads(rmsnorm(yc, lp['sb_norm_g']))
    y = jnp.concatenate([ya, yb, yc.astype(ya.dtype)], axis=-1) @ lp['w_out']
    x = x + gt1 * rmsnorm(y, lp['norm_g'][1])
    h = rmsnorm(x, lp['norm_g'][2]) * (1 + sc2) + sh2
    u, gv = split_cols(h @ lp['w_gate_up'], [D_FF, D_FF])
    uc, ffn_conv_new = causal_dwconv(u, ffn_conv_prev, lp['ffn_conv_w'])
    y = (jax.nn.gelu(uc, approximate=True) * gv) @ lp['w_down']
    x = x + gt2 * rmsnorm(y, lp['norm_g'][3])
    return x, (shift_new, s_rwkv_new, s_gdn_new, gdn_conv_new, ffn_conv_new, k, v)


def setup_inputs(seed: int = 0) -> dict:
    key = jax.random.key(seed)
    keys = iter(jax.random.split(key, 48))

    def nrm(shape, scale):
        return jax.random.normal(next(keys), shape, jnp.float32) * scale

    n_pages = PAST_LEN // PAGE_SIZE
    n_used = DEC_BATCH * n_pages
    n_pool = n_used + n_used // 4
    page_table = jax.random.permutation(next(keys), n_pool)[:n_used].reshape(DEC_BATCH, n_pages).astype(jnp.int32)
    dt = jnp.exp(jax.random.uniform(next(keys), (DEPTH, H_B), jnp.float32, float(np.log(1e-3)), float(np.log(1e-1))))
    return {
        'x_prompt': nrm((BATCH, SEQ, D_MODEL), 1.0),
        'x_sample': nrm((DEC_BATCH, DEC_SEQ, D_MODEL), 1.0),
        'state_rwkv': nrm((DEPTH, DEC_BATCH, H_A, D_HEAD, D_HEAD), 0.1),
        'state_rwkv_shift': nrm((DEPTH, DEC_BATCH, RWKV_COLS), 1.0),
        'state_gdn': nrm((DEPTH, DEC_BATCH, H_B, D_HEAD, D_HEAD), 0.1),
        'state_gdn_conv': nrm((DEPTH, DEC_BATCH, GDN_CONV - 1, GDN_QKV), 1.0),
        'cache_k': nrm((DEPTH, n_pool, PAGE_SIZE, H_C, D_HEAD), 1.0),
        'cache_v': nrm((DEPTH, n_pool, PAGE_SIZE, H_C, D_HEAD), 1.0),
        'state_ffn_conv': nrm((DEPTH, DEC_BATCH, FFN_CONV - 1, D_FF), 1.0),
        'page_table': page_table,
        'c_prompt': nrm((BATCH, D_MODEL), 1.0),
        'c_sample': nrm((DEC_BATCH, D_MODEL), 1.0),
        'w_ada': nrm((DEPTH, D_MODEL, N_MOD * D_MODEL), 0.5 * D_MODEL ** -0.5),
        'b_ada': nrm((DEPTH, N_MOD * D_MODEL), 0.01),
        'norm_g': 1.0 + nrm((DEPTH, 4, D_MODEL), 0.02),
        'w_in': nrm((DEPTH, D_MODEL, IN_COLS), D_MODEL ** -0.5),
        'rwkv_mu': jax.random.uniform(next(keys), (DEPTH, RWKV_COLS), jnp.float32),
        'rwkv_w0': jax.random.uniform(next(keys), (DEPTH, W_A), jnp.float32, -5.0, 0.0),
        'rwkv_w_up': nrm((DEPTH, RWKV_DECAY_RANK, W_A), 0.5 * RWKV_DECAY_RANK ** -0.5),
        'rwkv_a0': nrm((DEPTH, W_A), 0.1),
        'rwkv_a_up': nrm((DEPTH, RWKV_ICL_RANK, W_A), 0.5 * RWKV_ICL_RANK ** -0.5),
        'rwkv_g_up': nrm((DEPTH, RWKV_GATE_RANK, W_A), RWKV_GATE_RANK ** -0.5),
        'rwkv_k_k': 0.85 + nrm((DEPTH, W_A), 0.02),
        'rwkv_k_a': 1.0 + nrm((DEPTH, W_A), 0.02),
        'rwkv_r_k': nrm((DEPTH, H_A, D_HEAD), 0.1),
        'rwkv_ln_w': 1.0 + nrm((DEPTH, W_A), 0.02),
        'rwkv_ln_b': nrm((DEPTH, W_A), 0.01),
        'gdn_conv_w': nrm((DEPTH, GDN_CONV, GDN_QKV), 0.5),
        'gdn_a_log': jnp.log(jax.random.uniform(next(keys), (DEPTH, H_B), jnp.float32, 1.0, 16.0)),
        'gdn_dt_bias': dt + jnp.log(-jnp.expm1(-dt)),
        'gdn_norm_g': 1.0 + nrm((DEPTH, D_HEAD), 0.02),
        'sb_norm_g': 1.0 + nrm((DEPTH, D_HEAD), 0.02),
        'sb_bias': jax.random.uniform(next(keys), (DEPTH, H_C), jnp.float32, -9.0, -7.0),
        'w_out': nrm((DEPTH, D_MIX, D_MODEL), D_MIX ** -0.5),
        'w_gate_up': nrm((DEPTH, D_MODEL, 2 * D_FF), D_MODEL ** -0.5),
        'ffn_conv_w': nrm((DEPTH, FFN_CONV, D_FF), FFN_CONV ** -0.5),
        'w_down': nrm((DEPTH, D_FF, D_MODEL), D_FF ** -0.5),
    }


def reference(x_prompt, x_sample, state_rwkv, state_rwkv_shift, state_gdn, state_gdn_conv,
              cache_k, cache_v, state_ffn_conv, page_table, c_prompt, c_sample,
              w_ada, b_ada, norm_g, w_in, rwkv_mu, rwkv_w0, rwkv_w_up, rwkv_a0, rwkv_a_up,
              rwkv_g_up, rwkv_k_k, rwkv_k_a, rwkv_r_k, rwkv_ln_w, rwkv_ln_b, gdn_conv_w,
              gdn_a_log, gdn_dt_bias, gdn_norm_g, sb_norm_g, sb_bias, w_out, w_gate_up, ffn_conv_w, w_down):
    bp = x_prompt.shape[0]
    db = x_sample.shape[0]
    xp, xs = x_prompt, x_sample
    outs_p = [[] for _ in range(7)]
    outs_s = [[] for _ in range(7)]
    for l in range(DEPTH):
        lp = {
            'w_ada': w_ada[l], 'b_ada': b_ada[l], 'norm_g': norm_g[l], 'w_in': w_in[l],
            'rwkv_mu': rwkv_mu[l], 'rwkv_w0': rwkv_w0[l], 'rwkv_w_up': rwkv_w_up[l],
            'rwkv_a0': rwkv_a0[l], 'rwkv_a_up': rwkv_a_up[l], 'rwkv_g_up': rwkv_g_up[l],
            'rwkv_k_k': rwkv_k_k[l], 'rwkv_k_a': rwkv_k_a[l], 'rwkv_r_k': rwkv_r_k[l],
            'rwkv_ln_w': rwkv_ln_w[l], 'rwkv_ln_b': rwkv_ln_b[l], 'gdn_conv_w': gdn_conv_w[l],
            'gdn_a_log': gdn_a_log[l], 'gdn_dt_bias': gdn_dt_bias[l], 'gdn_norm_g': gdn_norm_g[l],
            'sb_norm_g': sb_norm_g[l], 'sb_bias': sb_bias[l], 'w_out': w_out[l],
            'w_gate_up': w_gate_up[l], 'ffn_conv_w': ffn_conv_w[l], 'w_down': w_down[l],
        }
        st_p = (jnp.zeros((bp, RWKV_COLS), xp.dtype),
                jnp.zeros((bp, H_A, D_HEAD, D_HEAD), jnp.float32),
                jnp.zeros((bp, H_B, D_HEAD, D_HEAD), jnp.float32),
                jnp.zeros((bp, GDN_CONV - 1, GDN_QKV), xp.dtype),
                jnp.zeros((bp, FFN_CONV - 1, D_FF), xp.dtype))
        xp, new_p = decoder_layer(xp, c_prompt, lp, st_p, None)
        k_past = cache_k[l][page_table].reshape(db, -1, H_C, D_HEAD)
        v_past = cache_v[l][page_table].reshape(db, -1, H_C, D_HEAD)
        st_s = (state_rwkv_shift[l], state_rwkv[l], state_gdn[l], state_gdn_conv[l], state_ffn_conv[l])
        xs, new_s = decoder_layer(xs, c_sample, lp, st_s, (k_past, v_past))
        for i in range(7):
            outs_p[i].append(new_p[i])
            outs_s[i].append(new_s[i])
    shift_p, rwkv_p, gdn_p, gconv_p, fconv_p, k_p, v_p = [jnp.stack(o, axis=0) for o in outs_p]
    shift_s, rwkv_s, gdn_s, gconv_s, fconv_s, k_s, v_s = [jnp.stack(o, axis=0) for o in outs_s]
    k_pages_p = k_p.reshape(DEPTH, bp, -1, PAGE_SIZE, H_C, D_HEAD)
    v_pages_p = v_p.reshape(DEPTH, bp, -1, PAGE_SIZE, H_C, D_HEAD)
    return (xp, xs, rwkv_p, rwkv_s, shift_p, shift_s, gdn_p, gdn_s, gconv_p, gconv_s,
            k_pages_p, k_s, v_pages_p, v_s, fconv_p, fconv_s)
```

```python
import functools

import numpy as np
import jax
import jax.numpy as jnp
from jax import lax
from jax.experimental import pallas as pl
from jax.experimental.pallas import tpu as pltpu

F32 = jnp.float32
BF16 = jnp.bfloat16

LANES = 128
SUBLANES = 8
HEAD = 64
NORM_EPS = 1e-6
L2_EPS = 1e-6
RWKV_LN_EPS = 64e-5
CHUNK = 64
ROW_TILE = 512
SB_Q_BLOCK = 256
SB_K_BLOCK = LANES
DEC_PAGES = 4
STEP_BATCH = 8


def _split2(x):
    hi = x.astype(BF16)
    lo = (x - hi.astype(F32)).astype(BF16)
    return hi, lo


def _split3(x):
    hi = x.astype(BF16)
    r = x - hi.astype(F32)
    mid = r.astype(BF16)
    lo = (r - mid.astype(F32)).astype(BF16)
    return hi, mid, lo


_NN = (((1,), (0,)), ((), ()))
_NT = (((1,), (1,)), ((), ()))
_TN = (((0,), (0,)), ((), ()))


def _mm(a, b, dims=_NN):
    return lax.dot_general(a, b, dims, preferred_element_type=F32)


def _bdot(a, b, dims=_NN):
    return _mm(a.astype(BF16), b.astype(BF16), dims)


def _dot3(a, b, dims=_NN):
    ah, al = _split2(a)
    bh, bl = _split2(b)
    return _mm(ah, bh, dims) + (_mm(ah, bl, dims) + _mm(al, bh, dims))


def _dot_exact_rhs(a, b_bf16, dims=_NN):
    h, m, l = _split3(a)
    return _mm(h, b_bf16, dims) + (_mm(m, b_bf16, dims) + _mm(l, b_bf16, dims))


def _dot_exact_lhs(a_bf16, b, dims=_NN):
    h, m, l = _split3(b)
    return _mm(a_bf16, h, dims) + (_mm(a_bf16, m, dims) + _mm(a_bf16, l, dims))


def _iota2(shape, dim):
    return lax.broadcasted_iota(jnp.int32, shape, dim)


def _ones_where(mask):
    return jnp.where(mask, 1.0, 0.0).astype(BF16)


def _head_of(idx):
    return jnp.right_shift(idx, HEAD.bit_length() - 1)


def _group_matrix(n):
    return _ones_where(_head_of(_iota2((n, n), 0)) == _head_of(_iota2((n, n), 1)))


def _head_sum(x):
    return _dot_exact_rhs(x, _group_matrix(x.shape[-1]))


def _softplus(x):
    return jnp.maximum(x, 0.0) + jnp.log1p(jnp.exp(-jnp.abs(x)))


def _silu(x):
    return x * jax.nn.sigmoid(x)


def _gelu_tanh(x):
    return 0.5 * x * (1.0 + jnp.tanh(0.7978845608028654 * (x + 0.044715 * (x * x * x))))


def _rms(x, eps=NORM_EPS):
    return x * lax.rsqrt(jnp.mean(x * x, axis=-1, keepdims=True) + eps)


def _shift_rows(x, prev8, k):
    rolled = pltpu.roll(x, k, 0)
    top = jnp.where(_iota2((SUBLANES, x.shape[1]), 0) < k, pltpu.roll(prev8, k, 0), rolled[:SUBLANES])
    if x.shape[0] == SUBLANES:
        return top
    return jnp.concatenate([top, rolled[SUBLANES:]], axis=0)


def _unit_lower_inverses(lows):
    n = lows[0].shape[0]
    row = _iota2((n, n), 0)
    col = _iota2((n, n), 1)
    same = lambda s: jnp.right_shift(row, s) == jnp.right_shift(col, s)
    eye = jnp.where(row == col, 1.0, 0.0)
    ld = [jnp.where(same(3), low, 0.0) for low in lows]
    p2 = [_dot3(x, x) for x in ld]
    p4 = [_dot3(x, x) for x in p2]
    t = [eye - x for x in ld]
    t = [x + _dot3(x, p) for x, p in zip(t, p2)]
    t = [x + _dot3(x, p) for x, p in zip(t, p4)]
    s = 3
    while (1 << s) < n:
        mask = same(s + 1) & jnp.logical_not(same(s))
        tl = [_dot3(x, jnp.where(mask, low, 0.0)) for x, low in zip(t, lows)]
        t = [x - _dot3(y, x) for x, y in zip(t, tl)]
        s += 1
    return t


def _modnorm(x, g, sc, sh):
    return _rms(x) * g * (1.0 + sc) + sh


def _ada_kernel(c_ref, w_ref, b_ref, o_ref):
    c = c_ref[...]
    o_ref[...] = _dot3(_silu(c), w_ref[...]) + b_ref[...]


def _ada_call(c_all, w_ada, b_ada):
    depth, d, n = w_ada.shape
    rows = c_all.shape[0]
    tn = 1024
    return pl.pallas_call(
        _ada_kernel,
        out_shape=jax.ShapeDtypeStruct((depth, rows, n), F32),
        grid=(depth, n // tn),
        in_specs=[
            pl.BlockSpec((rows, d), lambda l, j: (0, 0)),
            pl.BlockSpec((None, d, tn), lambda l, j: (l, 0, j)),
            pl.BlockSpec((None, 1, tn), lambda l, j: (l, 0, j)),
        ],
        out_specs=pl.BlockSpec((None, rows, tn), lambda l, j: (l, 0, j)),
        name="ada_mod",
    )(c_all, w_ada, b_ada.reshape(depth, 1, n))


_IN_GROUPS = ((0, 896), (896, 768), (1664, 256), (1920, 512), (2432, 512), (2944, 512), (3456, 128))
_IN_COLS = 3584


def _inproj_kernel(x_ref, g_ref, sc_ref, sh_ref, w_ref, *out_refs, q_scale):
    hb = _modnorm(x_ref[...], g_ref[...], sc_ref[...], sh_ref[...]).astype(BF16)
    groups = [_mm(hb, w_ref[:, lo:lo + wd]) for lo, wd in _IN_GROUPS]
    for val, o_ref in zip(groups, out_refs):
        o_ref[...] = val
    if len(out_refs) > len(_IN_GROUPS):
        qb_ref, kb_ref, vb_ref = out_refs[len(_IN_GROUPS):]
        qb_ref[...] = (groups[3] * q_scale).astype(BF16)
        kb_ref[...] = groups[4].astype(BF16)
        vb_ref[...] = groups[5].astype(BF16)


def _mod_spec(arr, tm, rows_per_seq):
    d = arr.shape[-1]
    if arr.shape[1] == 1:
        return pl.BlockSpec((None, 1, d), lambda i: ((i * tm) // rows_per_seq, 0, 0))
    return pl.BlockSpec((None, tm, d), lambda i: (0, i, 0))


def _inproj_call(x, norm_g, l, slot, sc, sh, w_in_b, tm, rows_per_seq, q_scale=None):
    m, d = x.shape
    outs = [((m, wd), F32) for _, wd in _IN_GROUPS]
    if q_scale is not None:
        outs += [((m, _IN_GROUPS[g][1]), BF16) for g in (3, 4, 5)]
    return pl.pallas_call(
        functools.partial(_inproj_kernel, q_scale=q_scale),
        out_shape=tuple(jax.ShapeDtypeStruct(s, t) for s, t in outs),
        grid=(m // tm,),
        in_specs=[
            pl.BlockSpec((tm, d), lambda i: (i, 0)),
            pl.BlockSpec((None, 1, d), lambda i: (4 * l + slot, 0, 0)),
            _mod_spec(sc, tm, rows_per_seq),
            _mod_spec(sh, tm, rows_per_seq),
            pl.BlockSpec((None, d, _IN_COLS), lambda i: (l, 0, 0)),
        ],
        out_specs=tuple(pl.BlockSpec((tm, s[1]), lambda i: (i, 0)) for s, _ in outs),
        name="in_proj",
    )(x, norm_g, sc, sh, w_in_b)


def _rwkv_mix(pa, prev, mu, vec, wlr):
    w = vec.shape[-1]
    xm = pa + mu * (prev - pa)
    r = xm[:, 0:w]
    k = xm[:, w:2 * w]
    v = xm[:, 2 * w:3 * w]
    t7 = xm[:, 3 * w:3 * w + LANES]
    w0, a0, k_k, k_a = vec[0:1], vec[1:2], vec[2:3], vec[3:4]
    r_k = vec[6:7]
    w_raw = w0 + _bdot(jnp.tanh(t7), wlr[0])
    lw = -jnp.exp(-_softplus(-w_raw) - 0.5)
    a = jax.nn.sigmoid(a0 + _bdot(t7, wlr[1]))
    gate = _bdot(jax.nn.sigmoid(t7), wlr[2])
    kkr = k * k_k
    kk = kkr * lax.rsqrt(_head_sum(kkr * kkr) + L2_EPS)
    k2 = k * (1.0 + (a - 1.0) * k_a)
    bonus = _head_sum(r * k2 * r_k) * v
    return r, lw, k2, v, kk, a, gate, bonus


def _rwkv_out(y, ln_w, ln_b, bonus, gate):
    mean = jnp.mean(y, axis=-1, keepdims=True)
    yc = y - mean
    var = jnp.mean(yc * yc, axis=-1, keepdims=True)
    return (yc * lax.rsqrt(var + RWKV_LN_EPS) * ln_w + ln_b + bonus) * gate


def _rwkv_chunk_kernel(pa_ref, shift_ref, s0_ref, mu_ref, vec_ref, wlr_ref, ya_ref, s_ref, carry_ref, *, heads):
    c = pa_ref.shape[0]

    @pl.when(pl.program_id(1) == 0)
    def _():
        carry_ref[...] = shift_ref[...]
        s_ref[...] = s0_ref[...]

    pa = pa_ref[...]
    prev = jnp.where(_iota2(pa.shape, 0) == 0, carry_ref[...], pltpu.roll(pa, 1, 0))
    carry_ref[...] = pa[c - 1:c, :]
    vec = vec_ref[...]
    r, lw, k2, v, kk, a, gate, bonus = _rwkv_mix(pa, prev, mu_ref[...], vec, wlr_ref[...])

    row = _iota2((c, c), 0)
    col = _iota2((c, c), 1)
    incl = row >= col
    strict = row > col
    g = _dot_exact_lhs(_ones_where(incl), lw)
    eg = jnp.exp(g)
    en = jnp.exp(-g)
    rt = r * eg
    kt = k2 * en
    at = kk * a * en
    bt = kk * jnp.exp(g - lw)
    egl = eg[c - 1:c, :]
    kg = kt * egl
    ag = at * egl
    ones_c = jnp.ones((c, HEAD), BF16)
    hs = [slice(h * HEAD, (h + 1) * HEAD) for h in range(heads)]
    s0 = [s_ref[h] for h in range(heads)]
    lhs2 = [jnp.concatenate([bt[:, sl], rt[:, sl]], axis=0) for sl in hs]
    m4 = [_bdot(lhs2[h], jnp.concatenate([kt[:, sl], at[:, sl]], axis=0), _NT) for h, sl in enumerate(hs)]
    proj = [_bdot(lhs2[h], s0[h]) for h in range(heads)]
    tinv = _unit_lower_inverses([jnp.where(strict, m4[h][:c, c:], 0.0) for h in range(heads)])
    rhs = [proj[h][:c] + _bdot(jnp.where(strict, m4[h][:c, :c], 0.0), v[:, sl]) for h, sl in enumerate(hs)]
    u = [_dot3(tinv[h], rhs[h]) for h in range(heads)]
    y = [proj[h][c:] + _bdot(jnp.where(incl, m4[h][c:, :c], 0.0), v[:, sl])
         - _bdot(jnp.where(incl, m4[h][c:, c:], 0.0), u[h]) for h, sl in enumerate(hs)]
    dec = [jnp.exp(_dot_exact_rhs(lw[:, sl], ones_c, _TN)) for sl in hs]
    s_new = [dec[h] * s0[h] + _bdot(kg[:, sl], v[:, sl], _TN) - _bdot(ag[:, sl], u[h], _TN)
             for h, sl in enumerate(hs)]
    out = [_rwkv_out(y[h], vec[4:5, sl], vec[5:6, sl], bonus[:, sl], gate[:, sl]) for h, sl in enumerate(hs)]
    s_ref[...] = jnp.stack(s_new, axis=0)
    ya_ref[...] = jnp.concatenate(out, axis=1)


def _rwkv_prompt_call(pa, shift_prev, s0, mu, vec, wlr, l, batch, seq):
    m, cols = pa.shape
    heads = s0.shape[1]
    w = heads * HEAD
    nc = seq // CHUNK
    return pl.pallas_call(
        functools.partial(_rwkv_chunk_kernel, heads=heads),
        out_shape=(jax.ShapeDtypeStruct((m, w), F32), jax.ShapeDtypeStruct(s0.shape, F32)),
        grid=(batch, nc),
        in_specs=[
            pl.BlockSpec((CHUNK, cols), lambda b, i: (b * nc + i, 0)),
            pl.BlockSpec((None, 1, cols), lambda b, i: (b, 0, 0)),
            pl.BlockSpec((None, heads, HEAD, HEAD), lambda b, i: (b, 0, 0, 0)),
            pl.BlockSpec((None, 1, cols), lambda b, i: (l, 0, 0)),
            pl.BlockSpec((None, 8, w), lambda b, i: (l, 0, 0)),
            pl.BlockSpec((None, 3, LANES, w), lambda b, i: (l, 0, 0, 0)),
        ],
        out_specs=(
            pl.BlockSpec((CHUNK, w), lambda b, i: (b * nc + i, 0)),
            pl.BlockSpec((None, heads, HEAD, HEAD), lambda b, i: (b, 0, 0, 0)),
        ),
        scratch_shapes=[pltpu.VMEM((1, cols), F32)],
        name="rwkv_chunk",
    )(pa, shift_prev, s0, mu, vec, wlr)


def _rwkv_prep_step_kernel(pa_ref, prev_ref, mu_ref, vec_ref, wlr_ref,
                           r_ref, w_ref, kk_ref, kka_ref, k_ref, v_ref, g_ref, b_ref):
    r, lw, k2, v, kk, a, gate, bonus = _rwkv_mix(pa_ref[...], prev_ref[...], mu_ref[...], vec_ref[...], wlr_ref[...])
    r_ref[...] = r
    w_ref[...] = jnp.exp(lw)
    kk_ref[...] = kk
    kka_ref[...] = kk * a
    k_ref[...] = k2
    v_ref[...] = v
    g_ref[...] = gate
    b_ref[...] = bonus


def _rwkv_prep_step_call(pa, prev, mu, vec, wlr, l):
    rows, cols = pa.shape
    w = vec.shape[-1]
    full = lambda shape: pl.BlockSpec(shape, lambda i: (0,) * len(shape))
    return pl.pallas_call(
        _rwkv_prep_step_kernel,
        out_shape=tuple(jax.ShapeDtypeStruct((rows, w), F32) for _ in range(8)),
        grid=(1,),
        in_specs=[
            full((rows, cols)), full((rows, cols)),
            pl.BlockSpec((None, 1, cols), lambda i: (l, 0, 0)),
            pl.BlockSpec((None, 8, w), lambda i: (l, 0, 0)),
            pl.BlockSpec((None, 3, LANES, w), lambda i: (l, 0, 0, 0)),
        ],
        out_specs=tuple(full((rows, w)) for _ in range(8)),
        name="rwkv_prep_step",
    )(pa, prev, mu, vec, wlr)


def _rwkv_state_step_kernel(s_ref, r_ref, w_ref, kk_ref, kka_ref, k_ref, v_ref, g_ref, b_ref, vec_ref,
                            so_ref, y_ref):
    bt, heads = s_ref.shape[0], s_ref.shape[1]
    vec = vec_ref[...]
    for i in range(bt):
        for h in range(heads):
            sl = slice(h * HEAD, (h + 1) * HEAD)
            s = s_ref[i, h]
            sa = jnp.sum(kk_ref[i, h] * s, axis=0, keepdims=True)
            s_new = w_ref[i, h] * s - kka_ref[i, h] * sa + k_ref[i, h] * v_ref[i, h]
            so_ref[i, h] = s_new
            y = jnp.sum(r_ref[i, h] * s_new, axis=0, keepdims=True)
            y_ref[i, h] = _rwkv_out(y, vec[4:5, sl], vec[5:6, sl], b_ref[i, h], g_ref[i, h])


def _rwkv_state_step_call(s, cols, rows_, vec, l):
    r_, heads = s.shape[0], s.shape[1]
    bt = STEP_BATCH
    w = heads * HEAD
    col_spec = pl.BlockSpec((bt, heads, HEAD, 1), lambda i: (i, 0, 0, 0))
    row_spec = pl.BlockSpec((bt, heads, 1, HEAD), lambda i: (i, 0, 0, 0))
    st_spec = pl.BlockSpec((bt, heads, HEAD, HEAD), lambda i: (i, 0, 0, 0))
    return pl.pallas_call(
        _rwkv_state_step_kernel,
        out_shape=(jax.ShapeDtypeStruct(s.shape, F32), jax.ShapeDtypeStruct((r_, heads, 1, HEAD), F32)),
        grid=(r_ // bt,),
        in_specs=[st_spec] + [col_spec] * 5 + [row_spec] * 3 + [pl.BlockSpec((None, 8, w), lambda i: (l, 0, 0))],
        out_specs=(st_spec, row_spec),
        name="rwkv_state_step",
    )(s, *cols, *rows_, vec)


def _gdn_gates(ab, alog, dtb):
    glog = -jnp.exp(alog) * _softplus(ab + dtb)
    beta = jax.nn.sigmoid(ab)
    return glog, beta


def _gdn_qkv(conv_out, scale):
    w = conv_out.shape[-1] // 3
    act = _silu(conv_out)
    q = act[:, 0:w]
    k = act[:, w:2 * w]
    v = act[:, 2 * w:3 * w]
    qn = q * lax.rsqrt(_head_sum(q * q) + L2_EPS) * scale
    kn = k * lax.rsqrt(_head_sum(k * k) + L2_EPS)
    return qn, kn, v


def _gdn_chunk_kernel(x_ref, z_ref, ab_ref, conv0_ref, s0_ref, cw_ref, gp_ref, ng_ref,
                      yb_ref, s_ref, carry_ref, *, heads, scale):
    c = x_ref.shape[0]

    @pl.when(pl.program_id(1) == 0)
    def _():
        carry_ref[...] = jnp.zeros(carry_ref.shape, F32)
        carry_ref[SUBLANES - 3:SUBLANES, :] = conv0_ref[...]
        s_ref[...] = s0_ref[...]

    x = x_ref[...]
    prev8 = carry_ref[...]
    cw = cw_ref[...]
    conv = x * cw[3:4]
    for d in (1, 2, 3):
        conv = conv + _shift_rows(x, prev8, d) * cw[3 - d:4 - d]
    carry_ref[...] = x[c - SUBLANES:c, :]
    qn, kn, v = _gdn_qkv(conv, scale)
    gp = gp_ref[...]
    glog, beta = _gdn_gates(ab_ref[...], gp[0:1], gp[1:2])

    row = _iota2((c, c), 0)
    col = _iota2((c, c), 1)
    incl = row >= col
    strict = row > col
    gc = _dot_exact_lhs(_ones_where(incl), glog)
    gct = _dot_exact_rhs(glog, _ones_where(row <= col), _TN)
    ng = ng_ref[...]
    zg = _silu(z_ref[...])
    hs = [slice(h * HEAD, (h + 1) * HEAD) for h in range(heads)]
    s0 = [s_ref[h] for h in range(heads)]
    gcol = [gc[:, h:h + 1] for h in range(heads)]
    bcol = [beta[:, heads + h:heads + h + 1] for h in range(heads)]
    dec = [jnp.where(incl, jnp.exp(jnp.where(incl, gcol[h] - gct[h:h + 1, :], 0.0)), 0.0) for h in range(heads)]
    kb = [kn[:, sl] * bcol[h] for h, sl in enumerate(hs)]
    m2 = [_bdot(jnp.concatenate([kb[h], qn[:, sl]], axis=0), kn[:, sl], _NT) for h, sl in enumerate(hs)]
    eg = [jnp.exp(gcol[h]) for h in range(heads)]
    ws = [_bdot(jnp.concatenate([kb[h] * eg[h], qn[:, sl] * eg[h]], axis=0), s0[h]) for h, sl in enumerate(hs)]
    tinv = _unit_lower_inverses([jnp.where(strict, m2[h][:c] * dec[h], 0.0) for h in range(heads)])
    vn = [_dot3(tinv[h], v[:, sl] * bcol[h] - ws[h][:c]) for h, sl in enumerate(hs)]
    o = [ws[h][c:] + _bdot(m2[h][c:] * dec[h], vn[h]) for h in range(heads)]
    glast = [gcol[h][c - 1:c, :] for h in range(heads)]
    s_new = [s0[h] * jnp.exp(glast[h]) + _bdot(kn[:, sl] * jnp.exp(glast[h] - gcol[h]), vn[h], _TN)
             for h, sl in enumerate(hs)]
    for h in range(heads):
        s_ref[h] = s_new[h]
    yb_ref[...] = jnp.concatenate([_rms(o[h]) * ng for h in range(heads)], axis=1) * zg


def _gdn_prompt_call(qkv, z, ab, conv0, s0, cw, gp, ng, l, batch, seq, scale):
    m, cols = qkv.shape
    heads = s0.shape[1]
    w = heads * HEAD
    nc = seq // CHUNK
    return pl.pallas_call(
        functools.partial(_gdn_chunk_kernel, heads=heads, scale=scale),
        out_shape=(jax.ShapeDtypeStruct((m, w), F32), jax.ShapeDtypeStruct(s0.shape, F32)),
        grid=(batch, nc),
        in_specs=[
            pl.BlockSpec((CHUNK, cols), lambda b, i: (b * nc + i, 0)),
            pl.BlockSpec((CHUNK, w), lambda b, i: (b * nc + i, 0)),
            pl.BlockSpec((CHUNK, LANES), lambda b, i: (b * nc + i, 0)),
            pl.BlockSpec((None, 3, cols), lambda b, i: (b, 0, 0)),
            pl.BlockSpec((None, heads, HEAD, HEAD), lambda b, i: (b, 0, 0, 0)),
            pl.BlockSpec((None, 4, cols), lambda b, i: (l, 0, 0)),
            pl.BlockSpec((None, 2, LANES), lambda b, i: (l, 0, 0)),
            pl.BlockSpec((None, 1, HEAD), lambda b, i: (l, 0, 0)),
        ],
        out_specs=(
            pl.BlockSpec((CHUNK, w), lambda b, i: (b * nc + i, 0)),
            pl.BlockSpec((None, heads, HEAD, HEAD), lambda b, i: (b, 0, 0, 0)),
        ),
        scratch_shapes=[pltpu.VMEM((SUBLANES, cols), F32)],
        name="gdn_chunk",
    )(qkv, z, ab, conv0, s0, cw, gp, ng)


def _gdn_prep_step_kernel(x_ref, buf_ref, z_ref, ab_ref, cw_ref, gp_ref,
                          q_ref, k_ref, v_ref, dec_ref, beta_ref, zs_ref, *, heads, scale):
    cw = cw_ref[...]
    conv = x_ref[...] * cw[3:4]
    for d in range(3):
        conv = conv + buf_ref[d] * cw[d:d + 1]
    qn, kn, v = _gdn_qkv(conv, scale)
    gp = gp_ref[...]
    glog, beta = _gdn_gates(ab_ref[...], gp[0:1], gp[1:2])
    w = heads * HEAD
    lane = _iota2((LANES, w), 0)
    head_of = _head_of(_iota2((LANES, w), 1))
    q_ref[...] = qn
    k_ref[...] = kn
    v_ref[...] = v
    dec_ref[...] = _dot_exact_rhs(jnp.exp(glog), _ones_where(lane == head_of))
    beta_ref[...] = _dot_exact_rhs(beta, _ones_where(lane == head_of + heads))
    zs_ref[...] = _silu(z_ref[...])


def _gdn_prep_step_call(x, bufs, z, ab, cw, gp, l, heads, scale):
    rows, cols = x.shape
    w = heads * HEAD
    full = lambda shape: pl.BlockSpec(shape, lambda i: (0,) * len(shape))
    return pl.pallas_call(
        functools.partial(_gdn_prep_step_kernel, heads=heads, scale=scale),
        out_shape=tuple(jax.ShapeDtypeStruct((rows, w), F32) for _ in range(6)),
        grid=(1,),
        in_specs=[
            full((rows, cols)), full((3, rows, cols)), full((rows, w)), full((rows, LANES)),
            pl.BlockSpec((None, 4, cols), lambda i: (l, 0, 0)),
            pl.BlockSpec((None, 2, LANES), lambda i: (l, 0, 0)),
        ],
        out_specs=tuple(full((rows, w)) for _ in range(6)),
        name="gdn_prep_step",
    )(x, bufs, z, ab, cw, gp)


def _gdn_state_step_kernel(s_ref, q_ref, k_ref, v_ref, dec_ref, beta_ref, zs_ref, ng_ref, so_ref, y_ref):
    bt, heads = s_ref.shape[0], s_ref.shape[1]
    ng = ng_ref[...]
    for i in range(bt):
        for h in range(heads):
            s = s_ref[i, h] * dec_ref[i, h]
            kc = k_ref[i, h]
            vn = (v_ref[i, h] - jnp.sum(kc * s, axis=0, keepdims=True)) * beta_ref[i, h]
            s_new = s + kc * vn
            so_ref[i, h] = s_new
            o = jnp.sum(q_ref[i, h] * s_new, axis=0, keepdims=True)
            y_ref[i, h] = _rms(o) * ng * zs_ref[i, h]


def _gdn_state_step_call(s, cols, rows_, ng, l):
    r_, heads = s.shape[0], s.shape[1]
    bt = STEP_BATCH
    col_spec = pl.BlockSpec((bt, heads, HEAD, 1), lambda i: (i, 0, 0, 0))
    row_spec = pl.BlockSpec((bt, heads, 1, HEAD), lambda i: (i, 0, 0, 0))
    st_spec = pl.BlockSpec((bt, heads, HEAD, HEAD), lambda i: (i, 0, 0, 0))
    return pl.pallas_call(
        _gdn_state_step_kernel,
        out_shape=(jax.ShapeDtypeStruct(s.shape, F32), jax.ShapeDtypeStruct((r_, heads, 1, HEAD), F32)),
        grid=(r_ // bt,),
        in_specs=[st_spec] + [col_spec] * 2 + [row_spec] * 4 + [pl.BlockSpec((None, 1, HEAD), lambda i: (l, 0, 0))],
        out_specs=(st_spec, row_spec),
        name="gdn_state_step",
    )(s, *cols, *rows_, ng)


def _log_sigmoid_pair(z):
    lsn = -(jnp.maximum(z, 0.0) + jnp.log(1.0 + jnp.exp(-jnp.abs(z))))
    return z + lsn, lsn


def _suffix_matrix(n, copies=1):
    assert n & (n - 1) == 0
    r = _iota2((copies * n, n + LANES), 0) & (n - 1)
    c = _iota2((copies * n, n + LANES), 1)
    return _ones_where((r > c) | (c >= n))


def _sb_tile(q_ref, k_ref, v_ref, bias_ref, acc_ref, carry_ref, off, *, heads, layer, masked):
    bq, bk = q_ref.shape[0], k_ref.shape[0]
    cmat = _suffix_matrix(bk, copies=2)
    if masked:
        reach = _iota2((bq, bk), 1) < _iota2((bq, bk), 0) + off
    hs = [slice(h * HEAD, (h + 1) * HEAD) for h in range(heads)]
    carries = [carry_ref[h] for h in range(heads)]
    zs = [_mm(q_ref[:, sl], k_ref[:, sl], _NT) + bias_ref[layer, h] for h, sl in enumerate(hs)]
    ls = [_log_sigmoid_pair(z) for z in zs]
    lsn = [jnp.where(reach, p[1], 0.0) if masked else p[1] for p in ls]
    cs = [_mm(jnp.concatenate(_split2(x), axis=1), cmat) for x in lsn]
    wgt = [jnp.exp(ls[h][0] + cs[h][:, :bk] + carries[h]) for h in range(heads)]
    if masked:
        wgt = [jnp.where(reach, w, 0.0) for w in wgt]
    pv = [_mm(wgt[h].astype(BF16), v_ref[:, sl]) for h, sl in enumerate(hs)]
    acc_ref[...] += jnp.concatenate(pv, axis=1)
    for h in range(heads):
        carry_ref[h] = carries[h] + cs[h][:, bk:]


def _sb_prompt_kernel(qi_ref, kj_ref, fl_ref, bias_ref, q_ref, k_ref, v_ref, ng_ref, o_ref, acc_ref, carry_ref,
                      *, heads, layer):
    p = pl.program_id(1)
    kj = kj_ref[p]
    flags = fl_ref[p]
    bq, bk = q_ref.shape[0], k_ref.shape[0]
    off = qi_ref[p] * bq - kj * bk
    tile = functools.partial(_sb_tile, q_ref, k_ref, v_ref, bias_ref, acc_ref, carry_ref, off,
                             heads=heads, layer=layer)

    @pl.when((flags & 1) != 0)
    def _():
        acc_ref[...] = jnp.zeros(acc_ref.shape, F32)
        carry_ref[...] = jnp.zeros(carry_ref.shape, F32)

    @pl.when((flags & 2) != 0)
    def _():
        tile(masked=True)

    @pl.when((flags & 2) == 0)
    def _():
        tile(masked=False)

    @pl.when(kj == 0)
    def _():
        ng = ng_ref[...]
        o_ref[...] = jnp.concatenate(
            [_rms(acc_ref[:, h * HEAD:(h + 1) * HEAD]) * ng for h in range(heads)], axis=1)


def _sb_prompt_call(q, k, v, bias, ng, l, batch, seq):
    m, w = q.shape
    heads = w // HEAD
    bq = min(SB_Q_BLOCK, seq)
    bk = SB_K_BLOCK
    nq, nk = seq // bq, seq // bk
    qi_l, kj_l, fl_l = [], [], []
    for qi in range(nq):
        last = ((qi + 1) * bq - 2) // bk
        for kj in range(last, -1, -1):
            qi_l.append(qi)
            kj_l.append(kj)
            fl_l.append((1 if kj == last else 0) | (2 if (kj + 1) * bk > qi * bq else 0))
    tables = [jnp.asarray(np.array(t, np.int32)) for t in (qi_l, kj_l, fl_l)]
    grid_spec = pltpu.PrefetchScalarGridSpec(
        num_scalar_prefetch=3,
        grid=(batch, len(qi_l)),
        in_specs=[
            pl.BlockSpec(memory_space=pltpu.SMEM),
            pl.BlockSpec((bq, w), lambda b, p, qt, kt, ft: (b * nq + qt[p], 0)),
            pl.BlockSpec((bk, w), lambda b, p, qt, kt, ft: (b * nk + kt[p], 0)),
            pl.BlockSpec((bk, w), lambda b, p, qt, kt, ft: (b * nk + kt[p], 0)),
            pl.BlockSpec((None, 1, HEAD), lambda b, p, qt, kt, ft: (l, 0, 0)),
        ],
        out_specs=pl.BlockSpec((bq, w), lambda b, p, qt, kt, ft: (b * nq + qt[p], 0)),
        scratch_shapes=[pltpu.VMEM((bq, w), F32), pltpu.VMEM((heads, bq, LANES), F32)],
    )
    return pl.pallas_call(
        functools.partial(_sb_prompt_kernel, heads=heads, layer=l),
        out_shape=jax.ShapeDtypeStruct((m, w), F32),
        grid_spec=grid_spec,
        name="sb_prompt",
    )(*tables, bias, q, k, v, ng)


def _sb_decode_kernel(pt_ref, q_ref, *refs, heads, scale, group):
    k_refs, v_refs = refs[:group], refs[group:2 * group]
    bias_ref, ng_ref, o_ref, acc_ref, carry_ref = refs[2 * group:]
    step = pl.program_id(1)
    ps = k_refs[0].shape[0] // heads

    @pl.when(step == 0)
    def _():
        acc_ref[...] = jnp.zeros(acc_ref.shape, F32)
        carry_ref[...] = jnp.zeros(carry_ref.shape, F32)

    qb = (q_ref[...] * scale).astype(BF16)
    head_z = _iota2((heads, ps), 0)
    head_y = _iota2((heads, HEAD), 0)
    cmat = _suffix_matrix(ps)
    bias = bias_ref[...]
    zs = []
    for g in range(group):
        z = jnp.zeros((heads, ps), F32)
        for h in range(heads):
            kh = k_refs[g][pl.ds(h, ps, stride=heads), :].astype(BF16)
            z = jnp.where(head_z == h, _mm(qb, kh, _NT), z)
        zs.append(z + bias)
    ls = [_log_sigmoid_pair(z) for z in zs]
    cs = [_dot_exact_rhs(p[1], cmat) for p in ls]
    carry = carry_ref[...]
    wgt = []
    for g in range(group):
        wgt.append(jnp.exp(ls[g][0] + cs[g][:, :ps] + carry).astype(BF16))
        carry = carry + cs[g][:, ps:]
    ys = [[_mm(wgt[g], v_refs[g][pl.ds(h, ps, stride=heads), :].astype(BF16)) for h in range(heads)]
          for g in range(group)]
    acc = acc_ref[...]
    for h in range(heads):
        yh = ys[0][h]
        for g in range(1, group):
            yh = yh + ys[g][h]
        acc = jnp.where(head_y == h, acc + yh, acc)
    acc_ref[...] = acc
    carry_ref[...] = carry

    @pl.when(step == pl.num_programs(1) - 1)
    def _():
        o_ref[...] = _rms(acc) * ng_ref[...]


def _sb_decode_call(q, cache_k, cache_v, page_table, bias_b, ng, l, scale):
    rows, w = q.shape
    heads = w // HEAD
    npages = page_table.shape[1]
    depth, npool, ps = cache_k.shape[0], cache_k.shape[1], cache_k.shape[2]
    group = DEC_PAGES if npages % DEC_PAGES == 0 else 1
    ck = cache_k.reshape(depth, npool, ps * heads, HEAD)
    cv = cache_v.reshape(depth, npool, ps * heads, HEAD)

    def page_spec(g):
        return pl.BlockSpec((None, None, ps * heads, HEAD),
                            lambda b, s, pt: (l, pt[b * npages + npages - 1 - (s * group + g)], 0, 0))

    grid_spec = pltpu.PrefetchScalarGridSpec(
        num_scalar_prefetch=1,
        grid=(rows, npages // group),
        in_specs=[pl.BlockSpec((None, heads, HEAD), lambda b, s, pt: (b, 0, 0))]
        + [page_spec(g) for g in range(group)] * 2
        + [pl.BlockSpec((None, heads, LANES), lambda b, s, pt: (l, 0, 0)),
           pl.BlockSpec((None, 1, HEAD), lambda b, s, pt: (l, 0, 0))],
        out_specs=pl.BlockSpec((None, heads, HEAD), lambda b, s, pt: (b, 0, 0)),
        scratch_shapes=[pltpu.VMEM((heads, HEAD), F32), pltpu.VMEM((heads, LANES), F32)],
    )
    out = pl.pallas_call(
        functools.partial(_sb_decode_kernel, heads=heads, scale=scale, group=group),
        out_shape=jax.ShapeDtypeStruct((rows, heads, HEAD), F32),
        grid_spec=grid_spec,
        name="sb_decode",
    )(page_table.reshape(-1), q.reshape(rows, heads, HEAD), *([ck] * group), *([cv] * group), bias_b, ng)
    return out.reshape(rows, w)


def _outproj_kernel(ya_ref, yb_ref, yc_ref, x_ref, gt_ref, g_ref, w_ref, o_ref):
    wa, wb = ya_ref.shape[1], yb_ref.shape[1]
    y = (_bdot(ya_ref[...], w_ref[0:wa, :]) + _bdot(yb_ref[...], w_ref[wa:wa + wb, :])
         + _bdot(yc_ref[...], w_ref[wa + wb:, :]))
    o_ref[...] = x_ref[...] + gt_ref[...] * (_rms(y) * g_ref[...])


def _outproj_call(ya, yb, yc, x, gt, norm_g, l, w_out_b, tm, rows_per_seq):
    m, d = x.shape
    row_spec = lambda a: pl.BlockSpec((tm, a.shape[1]), lambda i: (i, 0))
    return pl.pallas_call(
        _outproj_kernel,
        out_shape=jax.ShapeDtypeStruct((m, d), F32),
        grid=(m // tm,),
        in_specs=[
            row_spec(ya), row_spec(yb), row_spec(yc), row_spec(x),
            _mod_spec(gt, tm, rows_per_seq),
            pl.BlockSpec((None, 1, d), lambda i: (4 * l + 1, 0, 0)),
            pl.BlockSpec((None, w_out_b.shape[1], d), lambda i: (l, 0, 0)),
        ],
        out_specs=pl.BlockSpec((tm, d), lambda i: (i, 0)),
        name="out_proj",
    )(ya, yb, yc, x, gt, norm_g, w_out_b)


def _ffn_up_kernel(x_ref, g_ref, sc_ref, sh_ref, wu_ref, wg_ref, cw_ref, st_ref,
                   act_ref, tail_ref, h_ref, carry_ref, *, rows_per_seq):
    i = pl.program_id(0)
    j = pl.program_id(1)
    tm = x_ref.shape[0]

    @pl.when(j == 0)
    def _():
        h_ref[...] = _modnorm(x_ref[...], g_ref[...], sc_ref[...], sh_ref[...]).astype(BF16)

    @pl.when((i * tm) % rows_per_seq == 0)
    def _():
        carry_ref[j] = jnp.zeros(carry_ref.shape[1:], F32)
        carry_ref[j, SUBLANES - 2:SUBLANES, :] = st_ref[...]

    hb = h_ref[...]
    u = _mm(hb, wu_ref[...])
    gv = _mm(hb, wg_ref[...])
    prev8 = carry_ref[j]
    cw = cw_ref[...]
    uc = _shift_rows(u, prev8, 2) * cw[0:1] + _shift_rows(u, prev8, 1) * cw[1:2] + u * cw[2:3]
    tail = u[tm - SUBLANES:tm, :]
    carry_ref[j] = tail
    tail_ref[...] = tail
    act_ref[...] = _gelu_tanh(uc) * gv


def _ffn_up_call(x, norm_g, l, sc, sh, w_gu_b, cw, conv0, tm, batch, seq):
    m, d = x.shape
    dff = cw.shape[-1]
    tn = dff // 2
    nn = dff // tn
    return pl.pallas_call(
        functools.partial(_ffn_up_kernel, rows_per_seq=seq),
        out_shape=(jax.ShapeDtypeStruct((m, dff), F32), jax.ShapeDtypeStruct((m // tm, SUBLANES, dff), F32)),
        grid=(m // tm, nn),
        in_specs=[
            pl.BlockSpec((tm, d), lambda i, j: (i, 0)),
            pl.BlockSpec((None, 1, d), lambda i, j: (4 * l + 2, 0, 0)),
            pl.BlockSpec((None, 1, d), lambda i, j: ((i * tm) // seq, 0, 0)),
            pl.BlockSpec((None, 1, d), lambda i, j: ((i * tm) // seq, 0, 0)),
            pl.BlockSpec((None, d, tn), lambda i, j: (l, 0, j)),
            pl.BlockSpec((None, d, tn), lambda i, j: (l, 0, nn + j)),
            pl.BlockSpec((None, 3, tn), lambda i, j: (l, 0, j)),
            pl.BlockSpec((None, 2, tn), lambda i, j: ((i * tm) // seq, 0, j)),
        ],
        out_specs=(
            pl.BlockSpec((tm, tn), lambda i, j: (i, j)),
            pl.BlockSpec((None, SUBLANES, tn), lambda i, j: (i, 0, j)),
        ),
        scratch_shapes=[pltpu.VMEM((tm, d), BF16), pltpu.VMEM((nn, SUBLANES, tn), F32)],
        name="ffn_up",
    )(x, norm_g, sc, sh, w_gu_b, w_gu_b, cw, conv0)


def _ffn_up_step_kernel(x_ref, g_ref, sc_ref, sh_ref, wu_ref, wg_ref, cw_ref, buf_ref, act_ref, u_ref, h_ref):
    @pl.when(pl.program_id(0) == 0)
    def _():
        h_ref[...] = _modnorm(x_ref[...], g_ref[...], sc_ref[...], sh_ref[...]).astype(BF16)

    hb = h_ref[...]
    u = _mm(hb, wu_ref[...])
    gv = _mm(hb, wg_ref[...])
    cw = cw_ref[...]
    uc = buf_ref[0] * cw[0:1] + buf_ref[1] * cw[1:2] + u * cw[2:3]
    u_ref[...] = u
    act_ref[...] = _gelu_tanh(uc) * gv


def _ffn_up_step_call(x, norm_g, l, sc, sh, w_gu_b, cw, bufs):
    rows, d = x.shape
    dff = cw.shape[-1]
    tn = dff // 2
    nn = dff // tn
    return pl.pallas_call(
        _ffn_up_step_kernel,
        out_shape=(jax.ShapeDtypeStruct((rows, dff), F32), jax.ShapeDtypeStruct((rows, dff), F32)),
        grid=(nn,),
        in_specs=[
            pl.BlockSpec((rows, d), lambda j: (0, 0)),
            pl.BlockSpec((None, 1, d), lambda j: (4 * l + 2, 0, 0)),
            pl.BlockSpec((None, rows, d), lambda j: (0, 0, 0)),
            pl.BlockSpec((None, rows, d), lambda j: (0, 0, 0)),
            pl.BlockSpec((None, d, tn), lambda j: (l, 0, j)),
            pl.BlockSpec((None, d, tn), lambda j: (l, 0, nn + j)),
            pl.BlockSpec((None, 3, tn), lambda j: (l, 0, j)),
            pl.BlockSpec((2, rows, tn), lambda j: (0, 0, j)),
        ],
        out_specs=(pl.BlockSpec((rows, tn), lambda j: (0, j)), pl.BlockSpec((rows, tn), lambda j: (0, j))),
        scratch_shapes=[pltpu.VMEM((rows, d), BF16)],
        name="ffn_up_step",
    )(x, norm_g, sc, sh, w_gu_b, w_gu_b, cw, bufs)


def _ffn_down_kernel(a_ref, x_ref, gt_ref, g_ref, w_ref, o_ref):
    y = _bdot(a_ref[...], w_ref[...])
    o_ref[...] = x_ref[...] + gt_ref[...] * (_rms(y) * g_ref[...])


def _ffn_down_call(act, x, gt, norm_g, l, w_dn_b, tm, rows_per_seq):
    m, d = x.shape
    dff = act.shape[1]
    return pl.pallas_call(
        _ffn_down_kernel,
        out_shape=jax.ShapeDtypeStruct((m, d), F32),
        grid=(m // tm,),
        in_specs=[
            pl.BlockSpec((tm, dff), lambda i: (i, 0)),
            pl.BlockSpec((tm, d), lambda i: (i, 0)),
            _mod_spec(gt, tm, rows_per_seq),
            pl.BlockSpec((None, 1, d), lambda i: (4 * l + 3, 0, 0)),
            pl.BlockSpec((None, dff, d), lambda i: (l, 0, 0)),
        ],
        out_specs=pl.BlockSpec((tm, d), lambda i: (i, 0)),
        name="ffn_down",
    )(act, x, gt, norm_g, w_dn_b)


def kernel(x_prompt, x_sample, state_rwkv, state_rwkv_shift, state_gdn, state_gdn_conv, cache_k, cache_v, state_ffn_conv, page_table, c_prompt, c_sample, w_ada, b_ada, norm_g, w_in, rwkv_mu, rwkv_w0, rwkv_w_up, rwkv_a0, rwkv_a_up, rwkv_g_up, rwkv_k_k, rwkv_k_a, rwkv_r_k, rwkv_ln_w, rwkv_ln_b, gdn_conv_w, gdn_a_log, gdn_dt_bias, gdn_norm_g, sb_norm_g, sb_bias, w_out, w_gate_up, ffn_conv_w, w_down):
    bp, seq, d = x_prompt.shape
    rs = x_sample.shape[0]
    depth = w_ada.shape[0]
    h_a, h_b, h_c = state_rwkv.shape[2], state_gdn.shape[2], cache_k.shape[3]
    w_a, w_b, w_c = h_a * HEAD, h_b * HEAD, h_c * HEAD
    page = cache_k.shape[2]
    dff = w_down.shape[1]
    scale = HEAD ** -0.5
    rk_w, ri_w, rg_w = rwkv_w_up.shape[1], rwkv_a_up.shape[1], rwkv_g_up.shape[1]
    rwkv_cols = 3 * w_a + rk_w + ri_w + rg_w
    gdn_qkv = 3 * w_b
    gdn_cols = gdn_qkv + w_b + 2 * h_b
    assert rk_w + ri_w + rg_w == LANES and w_a == w_b and 2 * h_b <= LANES
    assert seq % CHUNK == 0 and rs % STEP_BATCH == 0
    assert (rwkv_cols, gdn_qkv, w_b, w_c, w_c, w_c) == tuple(wd for _, wd in _IN_GROUPS[:6])

    g_end = rwkv_cols + gdn_qkv + w_b
    ab_end = rwkv_cols + gdn_cols
    w_in_b = jnp.concatenate(
        [w_in[:, :, :g_end], w_in[:, :, ab_end:], w_in[:, :, g_end:ab_end],
         jnp.zeros((depth, d, LANES - 2 * h_b), F32)], axis=-1).astype(BF16)
    w_out_b = w_out.astype(BF16)
    w_gu_b = w_gate_up.astype(BF16)
    w_dn_b = w_down.astype(BF16)
    norm_g3 = norm_g.reshape(depth * 4, 1, d)
    wlr = jnp.stack([
        jnp.pad(rwkv_w_up, ((0, 0), (0, LANES - rk_w), (0, 0))),
        jnp.pad(rwkv_a_up, ((0, 0), (rk_w, LANES - rk_w - ri_w), (0, 0))),
        jnp.pad(rwkv_g_up, ((0, 0), (rk_w + ri_w, 0), (0, 0))),
    ], axis=1)
    vec = jnp.stack([rwkv_w0, rwkv_a0, rwkv_k_k, rwkv_k_a, rwkv_ln_w, rwkv_ln_b,
                     rwkv_r_k.reshape(depth, w_a), jnp.zeros((depth, w_a), F32)], axis=1)
    mu3 = rwkv_mu.reshape(depth, 1, rwkv_cols)
    gp = jnp.pad(jnp.stack([gdn_a_log, gdn_dt_bias], axis=1), ((0, 0), (0, 0), (0, LANES - h_b)))
    gdn_ng = gdn_norm_g.reshape(depth, 1, HEAD)
    sb_ng = sb_norm_g.reshape(depth, 1, HEAD)
    bias_b = jnp.broadcast_to(sb_bias[:, :, None], (depth, h_c, LANES))

    rows_c = bp + rs
    pad_c = (-rows_c) % SUBLANES
    c_all = jnp.concatenate([c_prompt, c_sample, jnp.zeros((pad_c, d), F32)], axis=0)
    mod = _ada_call(c_all, w_ada, b_ada)

    xp = x_prompt.reshape(bp * seq, d)
    xs = x_sample.reshape(rs, d)
    tm_p = min(ROW_TILE, seq)
    assert seq % tm_p == 0
    zeros_shift = jnp.zeros((bp, 1, rwkv_cols), F32)
    zeros_state_a = jnp.zeros((bp, h_a, HEAD, HEAD), F32)
    zeros_state_b = jnp.zeros((bp, h_b, HEAD, HEAD), F32)
    zeros_gconv = jnp.zeros((bp, 3, gdn_qkv), F32)
    zeros_fconv = jnp.zeros((bp, 2, dff), F32)

    outs_p = [[] for _ in range(7)]
    outs_s = [[] for _ in range(7)]
    for l in range(depth):
        mp = [mod[l, :bp, i * d:(i + 1) * d].reshape(bp, 1, d) for i in range(6)]
        ms = [mod[l, bp:bp + rs, i * d:(i + 1) * d].reshape(1, rs, d) for i in range(6)]

        pa, qkv, z, _, sk, sv, ab, q_b, k_b, v_b = _inproj_call(
            xp, norm_g3, l, 0, mp[1], mp[0], w_in_b, tm_p, seq, q_scale=scale)
        ya, rwkv_p = _rwkv_prompt_call(pa, zeros_shift, zeros_state_a, mu3, vec, wlr, l, bp, seq)
        yb, gdn_p = _gdn_prompt_call(qkv, z, ab, zeros_gconv, zeros_state_b, gdn_conv_w, gp, gdn_ng, l, bp, seq, scale)
        yc = _sb_prompt_call(q_b, k_b, v_b, sb_bias, sb_ng, l, bp, seq)
        xp = _outproj_call(ya, yb, yc, xp, mp[2], norm_g3, l, w_out_b, tm_p, seq)
        act, tail = _ffn_up_call(xp, norm_g3, l, mp[4], mp[3], w_gu_b, ffn_conv_w, zeros_fconv, tm_p, bp, seq)
        xp = _ffn_down_call(act, xp, mp[5], norm_g3, l, w_dn_b, tm_p, seq)
        outs_p[0].append(pa.reshape(bp, seq, rwkv_cols)[:, -1])
        outs_p[1].append(rwkv_p)
        outs_p[2].append(gdn_p)
        outs_p[3].append(qkv.reshape(bp, seq, gdn_qkv)[:, -3:])
        outs_p[4].append(tail.reshape(bp, seq // tm_p, SUBLANES, dff)[:, -1, SUBLANES - 2:])
        outs_p[5].append(sk.reshape(bp, seq // page, page, h_c, HEAD))
        outs_p[6].append(sv.reshape(bp, seq // page, page, h_c, HEAD))

        pa, qkv, z, sq, sk, sv, ab = _inproj_call(xs, norm_g3, l, 0, ms[1], ms[0], w_in_b, rs, 1)
        r, wd, kk, kka, k2, v, gate, bonus = _rwkv_prep_step_call(pa, state_rwkv_shift[l], mu3, vec, wlr, l)
        as_col = lambda t, hh: t.reshape(rs, hh, HEAD, 1)
        as_row = lambda t, hh: t.reshape(rs, hh, 1, HEAD)
        rwkv_s, ya = _rwkv_state_step_call(
            state_rwkv[l], [as_col(t, h_a) for t in (r, wd, kk, kka, k2)],
            [as_row(t, h_a) for t in (v, gate, bonus)], vec, l)
        gbuf = jnp.swapaxes(state_gdn_conv[l], 0, 1)
        qn, kn, gv_, dec, beta, zs = _gdn_prep_step_call(qkv, gbuf, z, ab, gdn_conv_w, gp, l, h_b, scale)
        gdn_s, yb = _gdn_state_step_call(
            state_gdn[l], [as_col(t, h_b) for t in (qn, kn)],
            [as_row(t, h_b) for t in (gv_, dec, beta, zs)], gdn_ng, l)
        yc = _sb_decode_call(sq, cache_k, cache_v, page_table, bias_b, sb_ng, l, scale)
        xs = _outproj_call(ya.reshape(rs, w_a), yb.reshape(rs, w_b), yc, xs, ms[2], norm_g3, l, w_out_b, rs, 1)
        fbuf = jnp.swapaxes(state_ffn_conv[l], 0, 1)
        act, u = _ffn_up_step_call(xs, norm_g3, l, ms[4], ms[3], w_gu_b, ffn_conv_w, fbuf)
        xs = _ffn_down_call(act, xs, ms[5], norm_g3, l, w_dn_b, rs, 1)
        outs_s[0].append(pa)
        outs_s[1].append(rwkv_s)
        outs_s[2].append(gdn_s)
        outs_s[3].append(jnp.concatenate([state_gdn_conv[l][:, 1:], qkv[:, None, :]], axis=1))
        outs_s[4].append(jnp.concatenate([state_ffn_conv[l][:, 1:], u[:, None, :]], axis=1))
        outs_s[5].append(sk.reshape(rs, 1, h_c, HEAD))
        outs_s[6].append(sv.reshape(rs, 1, h_c, HEAD))

    shift_p, rwkv_p, gdn_p, gconv_p, fconv_p, k_p, v_p = [jnp.stack(o, axis=0) for o in outs_p]
    shift_s, rwkv_s, gdn_s, gconv_s, fconv_s, k_s, v_s = [jnp.stack(o, axis=0) for o in outs_s]
    return (xp.reshape(bp, seq, d), xs.reshape(rs, 1, d), rwkv_p, rwkv_s, shift_p, shift_s, gdn_p, gdn_s,
            gconv_p, gconv_s, k_p, k_s, v_p, v_s, fconv_p, fconv_s)
```

```python
import functools

import numpy as np
import jax
import jax.numpy as jnp
from jax import lax
from jax.experimental import pallas as pl
from jax.experimental.pallas import tpu as pltpu

F32 = jnp.float32
BF16 = jnp.bfloat16

LANES = 128
SUBLANES = 8
HEAD = 64
NORM_EPS = 1e-6
L2_EPS = 1e-6
RWKV_LN_EPS = 64e-5
CHUNK = 64
CHUNKS_PER_STEP = 4
ROW_TILE = 512
SB_Q_BLOCK = 256
SB_K_BLOCK = LANES
SB_HEAD_GROUP = 8
DEC_PAGES = 8
STEP_BATCH = 8


def _split2(x):
    hi = x.astype(BF16)
    lo = (x - hi.astype(F32)).astype(BF16)
    return hi, lo


def _split3(x):
    hi = x.astype(BF16)
    r = x - hi.astype(F32)
    mid = r.astype(BF16)
    lo = (r - mid.astype(F32)).astype(BF16)
    return hi, mid, lo


_NN = (((1,), (0,)), ((), ()))
_NT = (((1,), (1,)), ((), ()))
_TN = (((0,), (0,)), ((), ()))


def _mm(a, b, dims=_NN):
    return lax.dot_general(a, b, dims, preferred_element_type=F32)


def _bdot(a, b, dims=_NN):
    return _mm(a.astype(BF16), b.astype(BF16), dims)


def _dot3(a, b, dims=_NN):
    ah, al = _split2(a)
    bh, bl = _split2(b)
    return _mm(ah, bh, dims) + (_mm(ah, bl, dims) + _mm(al, bh, dims))


def _dot_exact_rhs(a, b_bf16, dims=_NN):
    h, m, l = _split3(a)
    return _mm(h, b_bf16, dims) + (_mm(m, b_bf16, dims) + _mm(l, b_bf16, dims))


def _dot_exact_lhs(a_bf16, b, dims=_NN):
    h, m, l = _split3(b)
    return _mm(a_bf16, h, dims) + (_mm(a_bf16, m, dims) + _mm(a_bf16, l, dims))


def _iota2(shape, dim):
    return lax.broadcasted_iota(jnp.int32, shape, dim)


def _ones_where(mask):
    return jnp.where(mask, 1.0, 0.0).astype(BF16)


def _head_of(idx):
    return jnp.right_shift(idx, HEAD.bit_length() - 1)


def _group_matrix(n):
    return _ones_where(_head_of(_iota2((n, n), 0)) == _head_of(_iota2((n, n), 1)))


def _head_sum(x):
    return _dot_exact_rhs(x, _group_matrix(x.shape[-1]))


def _softplus(x):
    return jnp.maximum(x, 0.0) + jnp.log1p(jnp.exp(-jnp.abs(x)))


def _silu(x):
    return x * jax.nn.sigmoid(x)


def _gelu_tanh(x):
    return 0.5 * x * (1.0 + jnp.tanh(0.7978845608028654 * (x + 0.044715 * (x * x * x))))


def _rms(x, eps=NORM_EPS):
    return x * lax.rsqrt(jnp.mean(x * x, axis=-1, keepdims=True) + eps)


def _shift_rows(x, prev8, k):
    rolled = pltpu.roll(x, k, 0)
    top = jnp.where(_iota2((SUBLANES, x.shape[1]), 0) < k, pltpu.roll(prev8, k, 0), rolled[:SUBLANES])
    if x.shape[0] == SUBLANES:
        return top
    return jnp.concatenate([top, rolled[SUBLANES:]], axis=0)


def _unit_lower_inverses(lows):
    n = lows[0].shape[0]
    row = _iota2((n, n), 0)
    col = _iota2((n, n), 1)
    same = lambda s: jnp.right_shift(row, s) == jnp.right_shift(col, s)
    eye = jnp.where(row == col, 1.0, 0.0)
    ld = [jnp.where(same(3), low, 0.0) for low in lows]
    p2 = [_dot3(x, x) for x in ld]
    p4 = [_dot3(x, x) for x in p2]
    t = [eye - x for x in ld]
    t = [x + _dot3(x, p) for x, p in zip(t, p2)]
    t = [x + _dot3(x, p) for x, p in zip(t, p4)]
    s = 3
    while (1 << s) < n:
        mask = same(s + 1) & jnp.logical_not(same(s))
        tl = [_dot3(x, jnp.where(mask, low, 0.0)) for x, low in zip(t, lows)]
        t = [x - _dot3(y, x) for x, y in zip(t, tl)]
        s += 1
    return t


def _modnorm(x, g, sc, sh):
    return _rms(x) * g * (1.0 + sc) + sh


def _ada_kernel(c_ref, w_ref, b_ref, o_ref):
    c = c_ref[...]
    o_ref[...] = _dot3(_silu(c), w_ref[...]) + b_ref[...]


def _ada_call(c_all, w_ada, b_ada):
    depth, d, n = w_ada.shape
    rows = c_all.shape[0]
    tn = 1024
    return pl.pallas_call(
        _ada_kernel,
        out_shape=jax.ShapeDtypeStruct((depth, rows, n), F32),
        grid=(depth, n // tn),
        in_specs=[
            pl.BlockSpec((rows, d), lambda l, j: (0, 0)),
            pl.BlockSpec((None, d, tn), lambda l, j: (l, 0, j)),
            pl.BlockSpec((None, 1, tn), lambda l, j: (l, 0, j)),
        ],
        out_specs=pl.BlockSpec((None, rows, tn), lambda l, j: (l, 0, j)),
        name="ada_mod",
    )(c_all, w_ada, b_ada.reshape(depth, 1, n))


_IN_GROUPS = ((0, 896), (896, 768), (1664, 256), (1920, 512), (2432, 512), (2944, 512), (3456, 128))
_IN_COLS = 3584


def _inproj_kernel(x_ref, g_ref, sc_ref, sh_ref, w_ref, *out_refs, q_scale):
    hb = _modnorm(x_ref[...], g_ref[...], sc_ref[...], sh_ref[...]).astype(BF16)
    groups = [_mm(hb, w_ref[:, lo:lo + wd]) for lo, wd in _IN_GROUPS]
    for val, o_ref in zip(groups, out_refs):
        o_ref[...] = val
    if len(out_refs) > len(_IN_GROUPS):
        qb_ref, kb_ref, vb_ref = out_refs[len(_IN_GROUPS):]
        qb_ref[...] = (groups[3] * q_scale).astype(BF16)
        kb_ref[...] = groups[4].astype(BF16)
        vb_ref[...] = groups[5].astype(BF16)


def _mod_spec(arr, tm, rows_per_seq):
    d = arr.shape[-1]
    if arr.shape[1] == 1:
        return pl.BlockSpec((None, 1, d), lambda i: ((i * tm) // rows_per_seq, 0, 0))
    return pl.BlockSpec((None, tm, d), lambda i: (0, i, 0))


def _inproj_call(x, norm_g, l, slot, sc, sh, w_in_b, tm, rows_per_seq, q_scale=None):
    m, d = x.shape
    outs = [((m, wd), F32) for _, wd in _IN_GROUPS]
    if q_scale is not None:
        outs += [((m, _IN_GROUPS[g][1]), BF16) for g in (3, 4, 5)]
    return pl.pallas_call(
        functools.partial(_inproj_kernel, q_scale=q_scale),
        out_shape=tuple(jax.ShapeDtypeStruct(s, t) for s, t in outs),
        grid=(m // tm,),
        in_specs=[
            pl.BlockSpec((tm, d), lambda i: (i, 0)),
            pl.BlockSpec((None, 1, d), lambda i: (4 * l + slot, 0, 0)),
            _mod_spec(sc, tm, rows_per_seq),
            _mod_spec(sh, tm, rows_per_seq),
            pl.BlockSpec((None, d, _IN_COLS), lambda i: (l, 0, 0)),
        ],
        out_specs=tuple(pl.BlockSpec((tm, s[1]), lambda i: (i, 0)) for s, _ in outs),
        name="in_proj",
    )(x, norm_g, sc, sh, w_in_b)


def _rwkv_mix(pa, prev, mu, vec, wlr):
    w = vec.shape[-1]
    xm = pa + mu * (prev - pa)
    r = xm[:, 0:w]
    k = xm[:, w:2 * w]
    v = xm[:, 2 * w:3 * w]
    t7 = xm[:, 3 * w:3 * w + LANES]
    w0, a0, k_k, k_a = vec[0:1], vec[1:2], vec[2:3], vec[3:4]
    r_k = vec[6:7]
    w_raw = w0 + _bdot(jnp.tanh(t7), wlr[0])
    lw = -jnp.exp(-_softplus(-w_raw) - 0.5)
    a = jax.nn.sigmoid(a0 + _bdot(t7, wlr[1]))
    gate = _bdot(jax.nn.sigmoid(t7), wlr[2])
    kkr = k * k_k
    kk = kkr * lax.rsqrt(_head_sum(kkr * kkr) + L2_EPS)
    k2 = k * (1.0 + (a - 1.0) * k_a)
    bonus = _head_sum(r * k2 * r_k) * v
    return r, lw, k2, v, kk, a, gate, bonus


def _rwkv_out(y, ln_w, ln_b, bonus, gate):
    mean = jnp.mean(y, axis=-1, keepdims=True)
    yc = y - mean
    var = jnp.mean(yc * yc, axis=-1, keepdims=True)
    return (yc * lax.rsqrt(var + RWKV_LN_EPS) * ln_w + ln_b + bonus) * gate


def _rwkv_chunk_kernel(pa_ref, shift_ref, s0_ref, mu_ref, vec_ref, wlr_ref, ya_ref, s_ref, carry_ref, *, heads):
    rows = pa_ref.shape[0]
    c = CHUNK
    nch = rows // c

    @pl.when(pl.program_id(1) == 0)
    def _():
        carry_ref[...] = shift_ref[...]
        s_ref[...] = s0_ref[...]

    pa = pa_ref[...]
    prev = jnp.where(_iota2(pa.shape, 0) == 0, carry_ref[...], pltpu.roll(pa, 1, 0))
    carry_ref[...] = pa[rows - 1:rows, :]
    vec = vec_ref[...]
    r, lw, k2, v, kk, a, gate, bonus = _rwkv_mix(pa, prev, mu_ref[...], vec, wlr_ref[...])

    trow = _iota2((rows, rows), 0)
    tcol = _iota2((rows, rows), 1)
    same_chunk = jnp.right_shift(trow, c.bit_length() - 1) == jnp.right_shift(tcol, c.bit_length() - 1)
    g = _dot_exact_lhs(_ones_where(same_chunk & (trow >= tcol)), lw)
    eg = jnp.exp(g)
    en = jnp.exp(-g)
    rt = r * eg
    kt = k2 * en
    at = kk * a * en
    bt = kk * jnp.exp(g - lw)
    row = _iota2((c, c), 0)
    col = _iota2((c, c), 1)
    incl = row >= col
    strict = row > col
    ones_c = jnp.ones((c, HEAD), BF16)
    hs = [slice(h * HEAD, (h + 1) * HEAD) for h in range(heads)]
    cs = [slice(i * c, (i + 1) * c) for i in range(nch)]
    ch = [(i, h) for i in range(nch) for h in range(heads)]
    lhs2 = {(i, h): jnp.concatenate([bt[cs[i], hs[h]], rt[cs[i], hs[h]]], axis=0) for i, h in ch}
    m4 = {(i, h): _bdot(lhs2[i, h], jnp.concatenate([kt[cs[i], hs[h]], at[cs[i], hs[h]]], axis=0), _NT)
          for i, h in ch}
    tinv = dict(zip(ch, _unit_lower_inverses([jnp.where(strict, m4[q][:c, c:], 0.0) for q in ch])))
    lkv = {(i, h): _bdot(jnp.where(strict, m4[i, h][:c, :c], 0.0), v[cs[i], hs[h]]) for i, h in ch}
    rkv = {(i, h): _bdot(jnp.where(incl, m4[i, h][c:, :c], 0.0), v[cs[i], hs[h]]) for i, h in ch}
    dec = {(i, h): jnp.exp(_dot_exact_rhs(lw[cs[i], hs[h]], ones_c, _TN)) for i, h in ch}
    egl = [eg[(i + 1) * c - 1:(i + 1) * c, :] for i in range(nch)]
    kgv = {(i, h): _bdot(kt[cs[i], hs[h]] * egl[i][:, hs[h]], v[cs[i], hs[h]], _TN) for i, h in ch}
    agt = {(i, h): at[cs[i], hs[h]] * egl[i][:, hs[h]] for i, h in ch}
    state = [s_ref[h] for h in range(heads)]
    out_rows = []
    for i in range(nch):
        proj = [_bdot(lhs2[i, h], state[h]) for h in range(heads)]
        u = [_dot3(tinv[i, h], proj[h][:c] + lkv[i, h]) for h in range(heads)]
        y = [proj[h][c:] + rkv[i, h] - _bdot(jnp.where(incl, m4[i, h][c:, c:], 0.0), u[h]) for h in range(heads)]
        state = [dec[i, h] * state[h] + kgv[i, h] - _bdot(agt[i, h], u[h], _TN) for h in range(heads)]
        out_rows.append(jnp.concatenate(
            [_rwkv_out(y[h], vec[4:5, hs[h]], vec[5:6, hs[h]], bonus[cs[i], hs[h]], gate[cs[i], hs[h]])
             for h in range(heads)], axis=1))
    s_ref[...] = jnp.stack(state, axis=0)
    ya_ref[...] = jnp.concatenate(out_rows, axis=0)


def _rwkv_prompt_call(pa, shift_prev, s0, mu, vec, wlr, l, batch, seq):
    m, cols = pa.shape
    heads = s0.shape[1]
    w = heads * HEAD
    rows = min(CHUNK * CHUNKS_PER_STEP, seq)
    nc = seq // rows
    return pl.pallas_call(
        functools.partial(_rwkv_chunk_kernel, heads=heads),
        out_shape=(jax.ShapeDtypeStruct((m, w), F32), jax.ShapeDtypeStruct(s0.shape, F32)),
        grid=(batch, nc),
        in_specs=[
            pl.BlockSpec((rows, cols), lambda b, i: (b * nc + i, 0)),
            pl.BlockSpec((None, 1, cols), lambda b, i: (b, 0, 0)),
            pl.BlockSpec((None, heads, HEAD, HEAD), lambda b, i: (b, 0, 0, 0)),
            pl.BlockSpec((None, 1, cols), lambda b, i: (l, 0, 0)),
            pl.BlockSpec((None, 8, w), lambda b, i: (l, 0, 0)),
            pl.BlockSpec((None, 3, LANES, w), lambda b, i: (l, 0, 0, 0)),
        ],
        out_specs=(
            pl.BlockSpec((rows, w), lambda b, i: (b * nc + i, 0)),
            pl.BlockSpec((None, heads, HEAD, HEAD), lambda b, i: (b, 0, 0, 0)),
        ),
        scratch_shapes=[pltpu.VMEM((1, cols), F32)],
        name="rwkv_chunk",
    )(pa, shift_prev, s0, mu, vec, wlr)


def _rwkv_prep_step_kernel(pa_ref, prev_ref, mu_ref, vec_ref, wlr_ref,
                           r_ref, w_ref, kk_ref, kka_ref, k_ref, v_ref, g_ref, b_ref):
    r, lw, k2, v, kk, a, gate, bonus = _rwkv_mix(pa_ref[...], prev_ref[...], mu_ref[...], vec_ref[...], wlr_ref[...])
    r_ref[...] = r
    w_ref[...] = jnp.exp(lw)
    kk_ref[...] = kk
    kka_ref[...] = kk * a
    k_ref[...] = k2
    v_ref[...] = v
    g_ref[...] = gate
    b_ref[...] = bonus


def _rwkv_prep_step_call(pa, prev, mu, vec, wlr, l):
    rows, cols = pa.shape
    w = vec.shape[-1]
    full = lambda shape: pl.BlockSpec(shape, lambda i: (0,) * len(shape))
    return pl.pallas_call(
        _rwkv_prep_step_kernel,
        out_shape=tuple(jax.ShapeDtypeStruct((rows, w), F32) for _ in range(8)),
        grid=(1,),
        in_specs=[
            full((rows, cols)), full((rows, cols)),
            pl.BlockSpec((None, 1, cols), lambda i: (l, 0, 0)),
            pl.BlockSpec((None, 8, w), lambda i: (l, 0, 0)),
            pl.BlockSpec((None, 3, LANES, w), lambda i: (l, 0, 0, 0)),
        ],
        out_specs=tuple(full((rows, w)) for _ in range(8)),
        name="rwkv_prep_step",
    )(pa, prev, mu, vec, wlr)


def _rwkv_state_step_kernel(s_ref, r_ref, w_ref, kk_ref, kka_ref, k_ref, v_ref, g_ref, b_ref, vec_ref,
                            so_ref, y_ref):
    bt, heads = s_ref.shape[0], s_ref.shape[1]
    vec = vec_ref[...]
    for i in range(bt):
        for h in range(heads):
            sl = slice(h * HEAD, (h + 1) * HEAD)
            s = s_ref[i, h]
            sa = jnp.sum(kk_ref[i, h] * s, axis=0, keepdims=True)
            s_new = w_ref[i, h] * s - kka_ref[i, h] * sa + k_ref[i, h] * v_ref[i, h]
            so_ref[i, h] = s_new
            y = jnp.sum(r_ref[i, h] * s_new, axis=0, keepdims=True)
            y_ref[i, h] = _rwkv_out(y, vec[4:5, sl], vec[5:6, sl], b_ref[i, h], g_ref[i, h])


def _rwkv_state_step_call(s, cols, rows_, vec, l):
    r_, heads = s.shape[0], s.shape[1]
    bt = STEP_BATCH
    w = heads * HEAD
    col_spec = pl.BlockSpec((bt, heads, HEAD, 1), lambda i: (i, 0, 0, 0))
    row_spec = pl.BlockSpec((bt, heads, 1, HEAD), lambda i: (i, 0, 0, 0))
    st_spec = pl.BlockSpec((bt, heads, HEAD, HEAD), lambda i: (i, 0, 0, 0))
    return pl.pallas_call(
        _rwkv_state_step_kernel,
        out_shape=(jax.ShapeDtypeStruct(s.shape, F32), jax.ShapeDtypeStruct((r_, heads, 1, HEAD), F32)),
        grid=(r_ // bt,),
        in_specs=[st_spec] + [col_spec] * 5 + [row_spec] * 3 + [pl.BlockSpec((None, 8, w), lambda i: (l, 0, 0))],
        out_specs=(st_spec, row_spec),
        name="rwkv_state_step",
    )(s, *cols, *rows_, vec)


def _gdn_gates(ab, alog, dtb):
    glog = -jnp.exp(alog) * _softplus(ab + dtb)
    beta = jax.nn.sigmoid(ab)
    return glog, beta


def _gdn_qkv(conv_out, scale):
    w = conv_out.shape[-1] // 3
    act = _silu(conv_out)
    q = act[:, 0:w]
    k = act[:, w:2 * w]
    v = act[:, 2 * w:3 * w]
    qn = q * lax.rsqrt(_head_sum(q * q) + L2_EPS) * scale
    kn = k * lax.rsqrt(_head_sum(k * k) + L2_EPS)
    return qn, kn, v


def _gdn_chunk_kernel(x_ref, z_ref, ab_ref, conv0_ref, s0_ref, cw_ref, gp_ref, ng_ref,
                      yb_ref, s_ref, carry_ref, *, heads, scale):
    rows = x_ref.shape[0]
    c = CHUNK
    nch = rows // c

    @pl.when(pl.program_id(1) == 0)
    def _():
        carry_ref[...] = jnp.zeros(carry_ref.shape, F32)
        carry_ref[SUBLANES - 3:SUBLANES, :] = conv0_ref[...]
        s_ref[...] = s0_ref[...]

    x = x_ref[...]
    prev8 = carry_ref[...]
    cw = cw_ref[...]
    conv = x * cw[3:4]
    for d in (1, 2, 3):
        conv = conv + _shift_rows(x, prev8, d) * cw[3 - d:4 - d]
    carry_ref[...] = x[rows - SUBLANES:rows, :]
    qn, kn, v = _gdn_qkv(conv, scale)
    gp = gp_ref[...]
    glog, beta = _gdn_gates(ab_ref[...], gp[0:1], gp[1:2])

    trow = _iota2((rows, rows), 0)
    tcol = _iota2((rows, rows), 1)
    same_chunk = jnp.right_shift(trow, c.bit_length() - 1) == jnp.right_shift(tcol, c.bit_length() - 1)
    gc = _dot_exact_lhs(_ones_where(same_chunk & (trow >= tcol)), glog)
    gct = _dot_exact_rhs(glog, _ones_where(same_chunk & (trow <= tcol)), _TN)
    row = _iota2((c, c), 0)
    col = _iota2((c, c), 1)
    incl = row >= col
    strict = row > col
    ng = ng_ref[...]
    zg = _silu(z_ref[...])
    hs = [slice(h * HEAD, (h + 1) * HEAD) for h in range(heads)]
    cs = [slice(i * c, (i + 1) * c) for i in range(nch)]
    ch = [(i, h) for i in range(nch) for h in range(heads)]
    gcol = {(i, h): gc[cs[i], h:h + 1] for i, h in ch}
    bcol = {(i, h): beta[cs[i], heads + h:heads + h + 1] for i, h in ch}
    dec = {(i, h): jnp.where(incl, jnp.exp(jnp.where(incl, gcol[i, h] - gct[h:h + 1, cs[i]], 0.0)), 0.0)
           for i, h in ch}
    kb = {(i, h): kn[cs[i], hs[h]] * bcol[i, h] for i, h in ch}
    m2 = {(i, h): _bdot(jnp.concatenate([kb[i, h], qn[cs[i], hs[h]]], axis=0), kn[cs[i], hs[h]], _NT)
          for i, h in ch}
    tinv = dict(zip(ch, _unit_lower_inverses([jnp.where(strict, m2[q][:c] * dec[q], 0.0) for q in ch])))
    eg = {q: jnp.exp(gcol[q]) for q in ch}
    lhs = {(i, h): jnp.concatenate([kb[i, h] * eg[i, h], qn[cs[i], hs[h]] * eg[i, h]], axis=0) for i, h in ch}
    vb = {(i, h): v[cs[i], hs[h]] * bcol[i, h] for i, h in ch}
    qk = {q: m2[q][c:] * dec[q] for q in ch}
    glast = {q: gcol[q][c - 1:c, :] for q in ch}
    kd = {(i, h): kn[cs[i], hs[h]] * jnp.exp(glast[i, h] - gcol[i, h]) for i, h in ch}
    state = [s_ref[h] for h in range(heads)]
    out_rows = []
    for i in range(nch):
        ws = [_bdot(lhs[i, h], state[h]) for h in range(heads)]
        vn = [_dot3(tinv[i, h], vb[i, h] - ws[h][:c]) for h in range(heads)]
        o = [ws[h][c:] + _bdot(qk[i, h], vn[h]) for h in range(heads)]
        state = [state[h] * jnp.exp(glast[i, h]) + _bdot(kd[i, h], vn[h], _TN) for h in range(heads)]
        out_rows.append(jnp.concatenate([_rms(o[h]) * ng for h in range(heads)], axis=1))
    s_ref[...] = jnp.stack(state, axis=0)
    yb_ref[...] = jnp.concatenate(out_rows, axis=0) * zg


def _gdn_prompt_call(qkv, z, ab, conv0, s0, cw, gp, ng, l, batch, seq, scale):
    m, cols = qkv.shape
    heads = s0.shape[1]
    w = heads * HEAD
    rows = min(CHUNK * CHUNKS_PER_STEP, seq)
    nc = seq // rows
    return pl.pallas_call(
        functools.partial(_gdn_chunk_kernel, heads=heads, scale=scale),
        out_shape=(jax.ShapeDtypeStruct((m, w), F32), jax.ShapeDtypeStruct(s0.shape, F32)),
        grid=(batch, nc),
        in_specs=[
            pl.BlockSpec((rows, cols), lambda b, i: (b * nc + i, 0)),
            pl.BlockSpec((rows, w), lambda b, i: (b * nc + i, 0)),
            pl.BlockSpec((rows, LANES), lambda b, i: (b * nc + i, 0)),
            pl.BlockSpec((None, 3, cols), lambda b, i: (b, 0, 0)),
            pl.BlockSpec((None, heads, HEAD, HEAD), lambda b, i: (b, 0, 0, 0)),
            pl.BlockSpec((None, 4, cols), lambda b, i: (l, 0, 0)),
            pl.BlockSpec((None, 2, LANES), lambda b, i: (l, 0, 0)),
            pl.BlockSpec((None, 1, HEAD), lambda b, i: (l, 0, 0)),
        ],
        out_specs=(
            pl.BlockSpec((rows, w), lambda b, i: (b * nc + i, 0)),
            pl.BlockSpec((None, heads, HEAD, HEAD), lambda b, i: (b, 0, 0, 0)),
        ),
        scratch_shapes=[pltpu.VMEM((SUBLANES, cols), F32)],
        name="gdn_chunk",
    )(qkv, z, ab, conv0, s0, cw, gp, ng)


def _gdn_prep_step_kernel(x_ref, buf_ref, z_ref, ab_ref, cw_ref, gp_ref,
                          q_ref, k_ref, v_ref, dec_ref, beta_ref, zs_ref, *, heads, scale):
    cw = cw_ref[...]
    conv = x_ref[...] * cw[3:4]
    for d in range(3):
        conv = conv + buf_ref[d] * cw[d:d + 1]
    qn, kn, v = _gdn_qkv(conv, scale)
    gp = gp_ref[...]
    glog, beta = _gdn_gates(ab_ref[...], gp[0:1], gp[1:2])
    w = heads * HEAD
    lane = _iota2((LANES, w), 0)
    head_of = _head_of(_iota2((LANES, w), 1))
    q_ref[...] = qn
    k_ref[...] = kn
    v_ref[...] = v
    dec_ref[...] = _dot_exact_rhs(jnp.exp(glog), _ones_where(lane == head_of))
    beta_ref[...] = _dot_exact_rhs(beta, _ones_where(lane == head_of + heads))
    zs_ref[...] = _silu(z_ref[...])


def _gdn_prep_step_call(x, bufs, z, ab, cw, gp, l, heads, scale):
    rows, cols = x.shape
    w = heads * HEAD
    full = lambda shape: pl.BlockSpec(shape, lambda i: (0,) * len(shape))
    return pl.pallas_call(
        functools.partial(_gdn_prep_step_kernel, heads=heads, scale=scale),
        out_shape=tuple(jax.ShapeDtypeStruct((rows, w), F32) for _ in range(6)),
        grid=(1,),
        in_specs=[
            full((rows, cols)), full((3, rows, cols)), full((rows, w)), full((rows, LANES)),
            pl.BlockSpec((None, 4, cols), lambda i: (l, 0, 0)),
            pl.BlockSpec((None, 2, LANES), lambda i: (l, 0, 0)),
        ],
        out_specs=tuple(full((rows, w)) for _ in range(6)),
        name="gdn_prep_step",
    )(x, bufs, z, ab, cw, gp)


def _gdn_state_step_kernel(s_ref, q_ref, k_ref, v_ref, dec_ref, beta_ref, zs_ref, ng_ref, so_ref, y_ref):
    bt, heads = s_ref.shape[0], s_ref.shape[1]
    ng = ng_ref[...]
    for i in range(bt):
        for h in range(heads):
            s = s_ref[i, h] * dec_ref[i, h]
            kc = k_ref[i, h]
            vn = (v_ref[i, h] - jnp.sum(kc * s, axis=0, keepdims=True)) * beta_ref[i, h]
            s_new = s + kc * vn
            so_ref[i, h] = s_new
            o = jnp.sum(q_ref[i, h] * s_new, axis=0, keepdims=True)
            y_ref[i, h] = _rms(o) * ng * zs_ref[i, h]


def _gdn_state_step_call(s, cols, rows_, ng, l):
    r_, heads = s.shape[0], s.shape[1]
    bt = STEP_BATCH
    col_spec = pl.BlockSpec((bt, heads, HEAD, 1), lambda i: (i, 0, 0, 0))
    row_spec = pl.BlockSpec((bt, heads, 1, HEAD), lambda i: (i, 0, 0, 0))
    st_spec = pl.BlockSpec((bt, heads, HEAD, HEAD), lambda i: (i, 0, 0, 0))
    return pl.pallas_call(
        _gdn_state_step_kernel,
        out_shape=(jax.ShapeDtypeStruct(s.shape, F32), jax.ShapeDtypeStruct((r_, heads, 1, HEAD), F32)),
        grid=(r_ // bt,),
        in_specs=[st_spec] + [col_spec] * 2 + [row_spec] * 4 + [pl.BlockSpec((None, 1, HEAD), lambda i: (l, 0, 0))],
        out_specs=(st_spec, row_spec),
        name="gdn_state_step",
    )(s, *cols, *rows_, ng)


def _log_sigmoid_pair(z):
    lsn = -(jnp.maximum(z, 0.0) + jnp.log(1.0 + jnp.exp(-jnp.abs(z))))
    return z + lsn, lsn


def _suffix_matrix(n, copies=1):
    assert n & (n - 1) == 0
    r = _iota2((copies * n, n + LANES), 0) & (n - 1)
    c = _iota2((copies * n, n + LANES), 1)
    return _ones_where((r > c) | (c >= n))


def _sb_tile(q_ref, k_ref, v_ref, bias_ref, acc_ref, carry_ref, off, *, heads, layer, masked):
    bq, bk = q_ref.shape[0], k_ref.shape[0]
    cmat = _suffix_matrix(bk, copies=2)
    if masked:
        reach = _iota2((bq, bk), 1) < _iota2((bq, bk), 0) + off
    hs = [slice(h * HEAD, (h + 1) * HEAD) for h in range(heads)]
    carries = [carry_ref[h] for h in range(heads)]
    pv, new_carries = [], []
    for h0 in range(0, heads, SB_HEAD_GROUP):
        grp = range(h0, min(h0 + SB_HEAD_GROUP, heads))
        zs = [_mm(q_ref[:, hs[h]], k_ref[:, hs[h]], _NT) + bias_ref[layer, h] for h in grp]
        ls = [_log_sigmoid_pair(z) for z in zs]
        lsn = [jnp.where(reach, p[1], 0.0) if masked else p[1] for p in ls]
        cs = [_mm(jnp.concatenate(_split2(x), axis=1), cmat) for x in lsn]
        wgt = [jnp.exp(p[0] + c[:, :bk] + carries[h]) for p, c, h in zip(ls, cs, grp)]
        if masked:
            wgt = [jnp.where(reach, w, 0.0) for w in wgt]
        pv += [_mm(w.astype(BF16), v_ref[:, hs[h]]) for w, h in zip(wgt, grp)]
        new_carries += [carries[h] + c[:, bk:] for c, h in zip(cs, grp)]
    acc_ref[...] += jnp.concatenate(pv, axis=1)
    for h in range(heads):
        carry_ref[h] = new_carries[h]


def _sb_prompt_kernel(qi_ref, kj_ref, fl_ref, bias_ref, q_ref, k_ref, v_ref, ng_ref, o_ref, acc_ref, carry_ref,
                      *, heads, layer):
    p = pl.program_id(1)
    kj = kj_ref[p]
    flags = fl_ref[p]
    bq, bk = q_ref.shape[0], k_ref.shape[0]
    off = qi_ref[p] * bq - kj * bk
    tile = functools.partial(_sb_tile, q_ref, k_ref, v_ref, bias_ref, acc_ref, carry_ref, off,
                             heads=heads, layer=layer)

    @pl.when((flags & 1) != 0)
    def _():
        acc_ref[...] = jnp.zeros(acc_ref.shape, F32)
        carry_ref[...] = jnp.zeros(carry_ref.shape, F32)

    @pl.when((flags & 2) != 0)
    def _():
        tile(masked=True)

    @pl.when((flags & 2) == 0)
    def _():
        tile(masked=False)

    @pl.when(kj == 0)
    def _():
        ng = ng_ref[...]
        o_ref[...] = jnp.concatenate(
            [_rms(acc_ref[:, h * HEAD:(h + 1) * HEAD]) * ng for h in range(heads)], axis=1)


def _sb_prompt_call(q, k, v, bias, ng, l, batch, seq):
    m, w = q.shape
    heads = w // HEAD
    bq = min(SB_Q_BLOCK, seq)
    bk = SB_K_BLOCK
    nq, nk = seq // bq, seq // bk
    qi_l, kj_l, fl_l = [], [], []
    for qi in range(nq):
        last = ((qi + 1) * bq - 2) // bk
        for kj in range(last, -1, -1):
            qi_l.append(qi)
            kj_l.append(kj)
            fl_l.append((1 if kj == last else 0) | (2 if (kj + 1) * bk > qi * bq else 0))
    tables = [jnp.asarray(np.array(t, np.int32)) for t in (qi_l, kj_l, fl_l)]
    grid_spec = pltpu.PrefetchScalarGridSpec(
        num_scalar_prefetch=3,
        grid=(batch, len(qi_l)),
        in_specs=[
            pl.BlockSpec(memory_space=pltpu.SMEM),
            pl.BlockSpec((bq, w), lambda b, p, qt, kt, ft: (b * nq + qt[p], 0)),
            pl.BlockSpec((bk, w), lambda b, p, qt, kt, ft: (b * nk + kt[p], 0)),
            pl.BlockSpec((bk, w), lambda b, p, qt, kt, ft: (b * nk + kt[p], 0)),
            pl.BlockSpec((None, 1, HEAD), lambda b, p, qt, kt, ft: (l, 0, 0)),
        ],
        out_specs=pl.BlockSpec((bq, w), lambda b, p, qt, kt, ft: (b * nq + qt[p], 0)),
        scratch_shapes=[pltpu.VMEM((bq, w), F32), pltpu.VMEM((heads, bq, LANES), F32)],
    )
    return pl.pallas_call(
        functools.partial(_sb_prompt_kernel, heads=heads, layer=l),
        out_shape=jax.ShapeDtypeStruct((m, w), F32),
        grid_spec=grid_spec,
        name="sb_prompt",
    )(*tables, bias, q, k, v, ng)


def _sb_decode_kernel(pt_ref, q_ref, *refs, heads, scale, group):
    k_refs, v_refs = refs[:group], refs[group:2 * group]
    bias_ref, ng_ref, o_ref, acc_ref, carry_ref = refs[2 * group:]
    step = pl.program_id(1)
    ps = k_refs[0].shape[2]

    @pl.when(step == 0)
    def _():
        acc_ref[...] = jnp.zeros(acc_ref.shape, F32)
        carry_ref[...] = jnp.zeros(carry_ref.shape, F32)

    qb = (q_ref[...] * scale).astype(BF16)
    head_z = _iota2((heads, ps), 0)
    head_y = _iota2((heads, HEAD), 0)
    cmat = _suffix_matrix(ps)
    bias = bias_ref[...]
    zs = []
    for g in range(group):
        z = jnp.zeros((heads, ps), F32)
        for h in range(heads):
            z = jnp.where(head_z == h, _mm(qb, k_refs[g][h].astype(BF16)), z)
        zs.append(z + bias)
    ls = [_log_sigmoid_pair(z) for z in zs]
    cs = [_dot_exact_rhs(p[1], cmat) for p in ls]
    carry = carry_ref[...]
    wgt = []
    for g in range(group):
        wgt.append(jnp.exp(ls[g][0] + cs[g][:, :ps] + carry).astype(BF16))
        carry = carry + cs[g][:, ps:]
    ys = [[_mm(wgt[g], v_refs[g][h].astype(BF16), _NT) for h in range(heads)] for g in range(group)]
    acc = acc_ref[...]
    for h in range(heads):
        yh = ys[0][h]
        for g in range(1, group):
            yh = yh + ys[g][h]
        acc = jnp.where(head_y == h, acc + yh, acc)
    acc_ref[...] = acc
    carry_ref[...] = carry

    @pl.when(step == pl.num_programs(1) - 1)
    def _():
        o_ref[...] = _rms(acc) * ng_ref[...]


def _sb_decode_call(q, cache_k, cache_v, page_table, bias_b, ng, l, scale):
    rows, w = q.shape
    heads = w // HEAD
    npages = page_table.shape[1]
    depth, npool, ps = cache_k.shape[0], cache_k.shape[1], cache_k.shape[2]
    group = DEC_PAGES if npages % DEC_PAGES == 0 else 1
    ck = jnp.transpose(cache_k, (0, 1, 3, 4, 2))
    cv = jnp.transpose(cache_v, (0, 1, 3, 4, 2))

    def page_spec(g):
        return pl.BlockSpec((None, None, heads, HEAD, ps),
                            lambda b, s, pt: (l, pt[b * npages + npages - 1 - (s * group + g)], 0, 0, 0))

    grid_spec = pltpu.PrefetchScalarGridSpec(
        num_scalar_prefetch=1,
        grid=(rows, npages // group),
        in_specs=[pl.BlockSpec((None, heads, HEAD), lambda b, s, pt: (b, 0, 0))]
        + [page_spec(g) for g in range(group)] * 2
        + [pl.BlockSpec((None, heads, LANES), lambda b, s, pt: (l, 0, 0)),
           pl.BlockSpec((None, 1, HEAD), lambda b, s, pt: (l, 0, 0))],
        out_specs=pl.BlockSpec((None, heads, HEAD), lambda b, s, pt: (b, 0, 0)),
        scratch_shapes=[pltpu.VMEM((heads, HEAD), F32), pltpu.VMEM((heads, LANES), F32)],
    )
    out = pl.pallas_call(
        functools.partial(_sb_decode_kernel, heads=heads, scale=scale, group=group),
        out_shape=jax.ShapeDtypeStruct((rows, heads, HEAD), F32),
        grid_spec=grid_spec,
        name="sb_decode",
    )(page_table.reshape(-1), q.reshape(rows, heads, HEAD), *([ck] * group), *([cv] * group), bias_b, ng)
    return out.reshape(rows, w)


def _outproj_kernel(ya_ref, yb_ref, yc_ref, x_ref, gt_ref, g_ref, w_ref, o_ref):
    wa, wb = ya_ref.shape[1], yb_ref.shape[1]
    y = (_bdot(ya_ref[...], w_ref[0:wa, :]) + _bdot(yb_ref[...], w_ref[wa:wa + wb, :])
         + _bdot(yc_ref[...], w_ref[wa + wb:, :]))
    o_ref[...] = x_ref[...] + gt_ref[...] * (_rms(y) * g_ref[...])


def _outproj_call(ya, yb, yc, x, gt, norm_g, l, w_out_b, tm, rows_per_seq):
    m, d = x.shape
    row_spec = lambda a: pl.BlockSpec((tm, a.shape[1]), lambda i: (i, 0))
    return pl.pallas_call(
        _outproj_kernel,
        out_shape=jax.ShapeDtypeStruct((m, d), F32),
        grid=(m // tm,),
        in_specs=[
            row_spec(ya), row_spec(yb), row_spec(yc), row_spec(x),
            _mod_spec(gt, tm, rows_per_seq),
            pl.BlockSpec((None, 1, d), lambda i: (4 * l + 1, 0, 0)),
            pl.BlockSpec((None, w_out_b.shape[1], d), lambda i: (l, 0, 0)),
        ],
        out_specs=pl.BlockSpec((tm, d), lambda i: (i, 0)),
        name="out_proj",
    )(ya, yb, yc, x, gt, norm_g, w_out_b)


def _ffn_up_kernel(x_ref, g_ref, sc_ref, sh_ref, wu_ref, wg_ref, cw_ref, st_ref,
                   act_ref, tail_ref, h_ref, carry_ref, *, rows_per_seq):
    i = pl.program_id(0)
    j = pl.program_id(1)
    tm = x_ref.shape[0]

    @pl.when(j == 0)
    def _():
        h_ref[...] = _modnorm(x_ref[...], g_ref[...], sc_ref[...], sh_ref[...]).astype(BF16)

    @pl.when((i * tm) % rows_per_seq == 0)
    def _():
        carry_ref[j] = jnp.zeros(carry_ref.shape[1:], F32)
        carry_ref[j, SUBLANES - 2:SUBLANES, :] = st_ref[...]

    hb = h_ref[...]
    u = _mm(hb, wu_ref[...])
    gv = _mm(hb, wg_ref[...])
    prev8 = carry_ref[j]
    cw = cw_ref[...]
    uc = _shift_rows(u, prev8, 2) * cw[0:1] + _shift_rows(u, prev8, 1) * cw[1:2] + u * cw[2:3]
    tail = u[tm - SUBLANES:tm, :]
    carry_ref[j] = tail
    tail_ref[...] = tail
    act_ref[...] = _gelu_tanh(uc) * gv


def _ffn_up_call(x, norm_g, l, sc, sh, w_gu_b, cw, conv0, tm, batch, seq):
    m, d = x.shape
    dff = cw.shape[-1]
    tn = dff // 2
    nn = dff // tn
    return pl.pallas_call(
        functools.partial(_ffn_up_kernel, rows_per_seq=seq),
        out_shape=(jax.ShapeDtypeStruct((m, dff), F32), jax.ShapeDtypeStruct((m // tm, SUBLANES, dff), F32)),
        grid=(m // tm, nn),
        in_specs=[
            pl.BlockSpec((tm, d), lambda i, j: (i, 0)),
            pl.BlockSpec((None, 1, d), lambda i, j: (4 * l + 2, 0, 0)),
            pl.BlockSpec((None, 1, d), lambda i, j: ((i * tm) // seq, 0, 0)),
            pl.BlockSpec((None, 1, d), lambda i, j: ((i * tm) // seq, 0, 0)),
            pl.BlockSpec((None, d, tn), lambda i, j: (l, 0, j)),
            pl.BlockSpec((None, d, tn), lambda i, j: (l, 0, nn + j)),
            pl.BlockSpec((None, 3, tn), lambda i, j: (l, 0, j)),
            pl.BlockSpec((None, 2, tn), lambda i, j: ((i * tm) // seq, 0, j)),
        ],
        out_specs=(
            pl.BlockSpec((tm, tn), lambda i, j: (i, j)),
            pl.BlockSpec((None, SUBLANES, tn), lambda i, j: (i, 0, j)),
        ),
        scratch_shapes=[pltpu.VMEM((tm, d), BF16), pltpu.VMEM((nn, SUBLANES, tn), F32)],
        name="ffn_up",
    )(x, norm_g, sc, sh, w_gu_b, w_gu_b, cw, conv0)


def _ffn_up_step_kernel(x_ref, g_ref, sc_ref, sh_ref, wu_ref, wg_ref, cw_ref, buf_ref, act_ref, u_ref, h_ref):
    @pl.when(pl.program_id(0) == 0)
    def _():
        h_ref[...] = _modnorm(x_ref[...], g_ref[...], sc_ref[...], sh_ref[...]).astype(BF16)

    hb = h_ref[...]
    u = _mm(hb, wu_ref[...])
    gv = _mm(hb, wg_ref[...])
    cw = cw_ref[...]
    uc = buf_ref[0] * cw[0:1] + buf_ref[1] * cw[1:2] + u * cw[2:3]
    u_ref[...] = u
    act_ref[...] = _gelu_tanh(uc) * gv


def _ffn_up_step_call(x, norm_g, l, sc, sh, w_gu_b, cw, bufs):
    rows, d = x.shape
    dff = cw.shape[-1]
    tn = dff // 2
    nn = dff // tn
    return pl.pallas_call(
        _ffn_up_step_kernel,
        out_shape=(jax.ShapeDtypeStruct((rows, dff), F32), jax.ShapeDtypeStruct((rows, dff), F32)),
        grid=(nn,),
        in_specs=[
            pl.BlockSpec((rows, d), lambda j: (0, 0)),
            pl.BlockSpec((None, 1, d), lambda j: (4 * l + 2, 0, 0)),
            pl.BlockSpec((None, rows, d), lambda j: (0, 0, 0)),
            pl.BlockSpec((None, rows, d), lambda j: (0, 0, 0)),
            pl.BlockSpec((None, d, tn), lambda j: (l, 0, j)),
            pl.BlockSpec((None, d, tn), lambda j: (l, 0, nn + j)),
            pl.BlockSpec((None, 3, tn), lambda j: (l, 0, j)),
            pl.BlockSpec((2, rows, tn), lambda j: (0, 0, j)),
        ],
        out_specs=(pl.BlockSpec((rows, tn), lambda j: (0, j)), pl.BlockSpec((rows, tn), lambda j: (0, j))),
        scratch_shapes=[pltpu.VMEM((rows, d), BF16)],
        name="ffn_up_step",
    )(x, norm_g, sc, sh, w_gu_b, w_gu_b, cw, bufs)


def _ffn_down_kernel(a_ref, x_ref, gt_ref, g_ref, w_ref, o_ref):
    y = _bdot(a_ref[...], w_ref[...])
    o_ref[...] = x_ref[...] + gt_ref[...] * (_rms(y) * g_ref[...])


def _ffn_down_call(act, x, gt, norm_g, l, w_dn_b, tm, rows_per_seq):
    m, d = x.shape
    dff = act.shape[1]
    return pl.pallas_call(
        _ffn_down_kernel,
        out_shape=jax.ShapeDtypeStruct((m, d), F32),
        grid=(m // tm,),
        in_specs=[
            pl.BlockSpec((tm, dff), lambda i: (i, 0)),
            pl.BlockSpec((tm, d), lambda i: (i, 0)),
            _mod_spec(gt, tm, rows_per_seq),
            pl.BlockSpec((None, 1, d), lambda i: (4 * l + 3, 0, 0)),
            pl.BlockSpec((None, dff, d), lambda i: (l, 0, 0)),
        ],
        out_specs=pl.BlockSpec((tm, d), lambda i: (i, 0)),
        name="ffn_down",
    )(act, x, gt, norm_g, w_dn_b)


def kernel(x_prompt, x_sample, state_rwkv, state_rwkv_shift, state_gdn, state_gdn_conv, cache_k, cache_v, state_ffn_conv, page_table, c_prompt, c_sample, w_ada, b_ada, norm_g, w_in, rwkv_mu, rwkv_w0, rwkv_w_up, rwkv_a0, rwkv_a_up, rwkv_g_up, rwkv_k_k, rwkv_k_a, rwkv_r_k, rwkv_ln_w, rwkv_ln_b, gdn_conv_w, gdn_a_log, gdn_dt_bias, gdn_norm_g, sb_norm_g, sb_bias, w_out, w_gate_up, ffn_conv_w, w_down):
    bp, seq, d = x_prompt.shape
    rs = x_sample.shape[0]
    depth = w_ada.shape[0]
    h_a, h_b, h_c = state_rwkv.shape[2], state_gdn.shape[2], cache_k.shape[3]
    w_a, w_b, w_c = h_a * HEAD, h_b * HEAD, h_c * HEAD
    page = cache_k.shape[2]
    dff = w_down.shape[1]
    scale = HEAD ** -0.5
    rk_w, ri_w, rg_w = rwkv_w_up.shape[1], rwkv_a_up.shape[1], rwkv_g_up.shape[1]
    rwkv_cols = 3 * w_a + rk_w + ri_w + rg_w
    gdn_qkv = 3 * w_b
    gdn_cols = gdn_qkv + w_b + 2 * h_b
    assert rk_w + ri_w + rg_w == LANES and w_a == w_b and 2 * h_b <= LANES
    assert seq % CHUNK == 0 and rs % STEP_BATCH == 0
    assert (rwkv_cols, gdn_qkv, w_b, w_c, w_c, w_c) == tuple(wd for _, wd in _IN_GROUPS[:6])

    g_end = rwkv_cols + gdn_qkv + w_b
    ab_end = rwkv_cols + gdn_cols
    w_in_b = jnp.concatenate(
        [w_in[:, :, :g_end], w_in[:, :, ab_end:], w_in[:, :, g_end:ab_end],
         jnp.zeros((depth, d, LANES - 2 * h_b), F32)], axis=-1).astype(BF16)
    w_out_b = w_out.astype(BF16)
    w_gu_b = w_gate_up.astype(BF16)
    w_dn_b = w_down.astype(BF16)
    norm_g3 = norm_g.reshape(depth * 4, 1, d)
    wlr = jnp.stack([
        jnp.pad(rwkv_w_up, ((0, 0), (0, LANES - rk_w), (0, 0))),
        jnp.pad(rwkv_a_up, ((0, 0), (rk_w, LANES - rk_w - ri_w), (0, 0))),
        jnp.pad(rwkv_g_up, ((0, 0), (rk_w + ri_w, 0), (0, 0))),
    ], axis=1)
    vec = jnp.stack([rwkv_w0, rwkv_a0, rwkv_k_k, rwkv_k_a, rwkv_ln_w, rwkv_ln_b,
                     rwkv_r_k.reshape(depth, w_a), jnp.zeros((depth, w_a), F32)], axis=1)
    mu3 = rwkv_mu.reshape(depth, 1, rwkv_cols)
    gp = jnp.pad(jnp.stack([gdn_a_log, gdn_dt_bias], axis=1), ((0, 0), (0, 0), (0, LANES - h_b)))
    gdn_ng = gdn_norm_g.reshape(depth, 1, HEAD)
    sb_ng = sb_norm_g.reshape(depth, 1, HEAD)
    bias_b = jnp.broadcast_to(sb_bias[:, :, None], (depth, h_c, LANES))

    rows_c = bp + rs
    pad_c = (-rows_c) % SUBLANES
    c_all = jnp.concatenate([c_prompt, c_sample, jnp.zeros((pad_c, d), F32)], axis=0)
    mod = _ada_call(c_all, w_ada, b_ada)

    xp = x_prompt.reshape(bp * seq, d)
    xs = x_sample.reshape(rs, d)
    tm_p = min(ROW_TILE, seq)
    assert seq % tm_p == 0
    zeros_shift = jnp.zeros((bp, 1, rwkv_cols), F32)
    zeros_state_a = jnp.zeros((bp, h_a, HEAD, HEAD), F32)
    zeros_state_b = jnp.zeros((bp, h_b, HEAD, HEAD), F32)
    zeros_gconv = jnp.zeros((bp, 3, gdn_qkv), F32)
    zeros_fconv = jnp.zeros((bp, 2, dff), F32)

    outs_p = [[] for _ in range(7)]
    outs_s = [[] for _ in range(7)]
    for l in range(depth):
        mp = [mod[l, :bp, i * d:(i + 1) * d].reshape(bp, 1, d) for i in range(6)]
        ms = [mod[l, bp:bp + rs, i * d:(i + 1) * d].reshape(1, rs, d) for i in range(6)]

        pa, qkv, z, _, sk, sv, ab, q_b, k_b, v_b = _inproj_call(
            xp, norm_g3, l, 0, mp[1], mp[0], w_in_b, tm_p, seq, q_scale=scale)
        ya, rwkv_p = _rwkv_prompt_call(pa, zeros_shift, zeros_state_a, mu3, vec, wlr, l, bp, seq)
        yb, gdn_p = _gdn_prompt_call(qkv, z, ab, zeros_gconv, zeros_state_b, gdn_conv_w, gp, gdn_ng, l, bp, seq, scale)
        yc = _sb_prompt_call(q_b, k_b, v_b, sb_bias, sb_ng, l, bp, seq)
        xp = _outproj_call(ya, yb, yc, xp, mp[2], norm_g3, l, w_out_b, tm_p, seq)
        act, tail = _ffn_up_call(xp, norm_g3, l, mp[4], mp[3], w_gu_b, ffn_conv_w, zeros_fconv, tm_p, bp, seq)
        xp = _ffn_down_call(act, xp, mp[5], norm_g3, l, w_dn_b, tm_p, seq)
        outs_p[0].append(pa.reshape(bp, seq, rwkv_cols)[:, -1])
        outs_p[1].append(rwkv_p)
        outs_p[2].append(gdn_p)
        outs_p[3].append(qkv.reshape(bp, seq, gdn_qkv)[:, -3:])
        outs_p[4].append(tail.reshape(bp, seq // tm_p, SUBLANES, dff)[:, -1, SUBLANES - 2:])
        outs_p[5].append(sk.reshape(bp, seq // page, page, h_c, HEAD))
        outs_p[6].append(sv.reshape(bp, seq // page, page, h_c, HEAD))

        pa, qkv, z, sq, sk, sv, ab = _inproj_call(xs, norm_g3, l, 0, ms[1], ms[0], w_in_b, rs, 1)
        r, wd, kk, kka, k2, v, gate, bonus = _rwkv_prep_step_call(pa, state_rwkv_shift[l], mu3, vec, wlr, l)
        as_col = lambda t, hh: t.reshape(rs, hh, HEAD, 1)
        as_row = lambda t, hh: t.reshape(rs, hh, 1, HEAD)
        rwkv_s, ya = _rwkv_state_step_call(
            state_rwkv[l], [as_col(t, h_a) for t in (r, wd, kk, kka, k2)],
            [as_row(t, h_a) for t in (v, gate, bonus)], vec, l)
        gbuf = jnp.swapaxes(state_gdn_conv[l], 0, 1)
        qn, kn, gv_, dec, beta, zs = _gdn_prep_step_call(qkv, gbuf, z, ab, gdn_conv_w, gp, l, h_b, scale)
        gdn_s, yb = _gdn_state_step_call(
            state_gdn[l], [as_col(t, h_b) for t in (qn, kn)],
            [as_row(t, h_b) for t in (gv_, dec, beta, zs)], gdn_ng, l)
        yc = _sb_decode_call(sq, cache_k, cache_v, page_table, bias_b, sb_ng, l, scale)
        xs = _outproj_call(ya.reshape(rs, w_a), yb.reshape(rs, w_b), yc, xs, ms[2], norm_g3, l, w_out_b, rs, 1)
        fbuf = jnp.swapaxes(state_ffn_conv[l], 0, 1)
        act, u = _ffn_up_step_call(xs, norm_g3, l, ms[4], ms[3], w_gu_b, ffn_conv_w, fbuf)
        xs = _ffn_down_call(act, xs, ms[5], norm_g3, l, w_dn_b, rs, 1)
        outs_s[0].append(pa)
        outs_s[1].append(rwkv_s)
        outs_s[2].append(gdn_s)
        outs_s[3].append(jnp.concatenate([state_gdn_conv[l][:, 1:], qkv[:, None, :]], axis=1))
        outs_s[4].append(jnp.concatenate([state_ffn_conv[l][:, 1:], u[:, None, :]], axis=1))
        outs_s[5].append(sk.reshape(rs, 1, h_c, HEAD))
        outs_s[6].append(sv.reshape(rs, 1, h_c, HEAD))

    shift_p, rwkv_p, gdn_p, gconv_p, fconv_p, k_p, v_p = [jnp.stack(o, axis=0) for o in outs_p]
    shift_s, rwkv_s, gdn_s, gconv_s, fconv_s, k_s, v_s = [jnp.stack(o, axis=0) for o in outs_s]
    return (xp.reshape(bp, seq, d), xs.reshape(rs, 1, d), rwkv_p, rwkv_s, shift_p, shift_s, gdn_p, gdn_s,
            gconv_p, gconv_s, k_p, k_s, v_p, v_s, fconv_p, fconv_s)
```

```python
import functools

import numpy as np
import jax
import jax.numpy as jnp
from jax import lax
from jax.experimental import pallas as pl
from jax.experimental.pallas import tpu as pltpu

F32 = jnp.float32
BF16 = jnp.bfloat16

LANES = 128
SUBLANES = 8
HEAD = 64
NORM_EPS = 1e-6
L2_EPS = 1e-6
RWKV_LN_EPS = 64e-5
CHUNK = 64
CHUNKS_PER_STEP = 4
ROW_TILE = 512
SB_Q_BLOCK = 256
SB_K_BLOCK = LANES
SB_HEAD_GROUP = 8
DEC_PAGES = 8


def _split2(x):
    hi = x.astype(BF16)
    lo = (x - hi.astype(F32)).astype(BF16)
    return hi, lo


def _split3(x):
    hi = x.astype(BF16)
    r = x - hi.astype(F32)
    mid = r.astype(BF16)
    lo = (r - mid.astype(F32)).astype(BF16)
    return hi, mid, lo


_NN = (((1,), (0,)), ((), ()))
_NT = (((1,), (1,)), ((), ()))
_TN = (((0,), (0,)), ((), ()))


def _mm(a, b, dims=_NN):
    return lax.dot_general(a, b, dims, preferred_element_type=F32)


def _bdot(a, b, dims=_NN):
    return _mm(a.astype(BF16), b.astype(BF16), dims)


def _dot3(a, b, dims=_NN):
    ah, al = _split2(a)
    bh, bl = _split2(b)
    return _mm(ah, bh, dims) + (_mm(ah, bl, dims) + _mm(al, bh, dims))


def _dot_exact_rhs(a, b_bf16, dims=_NN):
    h, m, l = _split3(a)
    return _mm(h, b_bf16, dims) + (_mm(m, b_bf16, dims) + _mm(l, b_bf16, dims))


def _dot_exact_lhs(a_bf16, b, dims=_NN):
    h, m, l = _split3(b)
    return _mm(a_bf16, h, dims) + (_mm(a_bf16, m, dims) + _mm(a_bf16, l, dims))


def _iota2(shape, dim):
    return lax.broadcasted_iota(jnp.int32, shape, dim)


def _ones_where(mask):
    return jnp.where(mask, 1.0, 0.0).astype(BF16)


def _head_of(idx):
    return jnp.right_shift(idx, HEAD.bit_length() - 1)


def _group_matrix(n):
    return _ones_where(_head_of(_iota2((n, n), 0)) == _head_of(_iota2((n, n), 1)))


def _head_sum(x):
    return _dot_exact_rhs(x, _group_matrix(x.shape[-1]))


def _softplus(x):
    return jnp.maximum(x, 0.0) + jnp.log1p(jnp.exp(-jnp.abs(x)))


def _silu(x):
    return x * jax.nn.sigmoid(x)


def _gelu_tanh(x):
    return 0.5 * x * (1.0 + jnp.tanh(0.7978845608028654 * (x + 0.044715 * (x * x * x))))


def _rms(x, eps=NORM_EPS):
    return x * lax.rsqrt(jnp.mean(x * x, axis=-1, keepdims=True) + eps)


def _shift_rows(x, prev8, k):
    rolled = pltpu.roll(x, k, 0)
    top = jnp.where(_iota2((SUBLANES, x.shape[1]), 0) < k, pltpu.roll(prev8, k, 0), rolled[:SUBLANES])
    if x.shape[0] == SUBLANES:
        return top
    return jnp.concatenate([top, rolled[SUBLANES:]], axis=0)


def _unit_lower_inverses(lows):
    n = lows[0].shape[0]
    row = _iota2((n, n), 0)
    col = _iota2((n, n), 1)
    same = lambda s: jnp.right_shift(row, s) == jnp.right_shift(col, s)
    eye = jnp.where(row == col, 1.0, 0.0)
    ld = [jnp.where(same(3), low, 0.0) for low in lows]
    p2 = [_dot3(x, x) for x in ld]
    p4 = [_dot3(x, x) for x in p2]
    t = [eye - x for x in ld]
    t = [x + _dot3(x, p) for x, p in zip(t, p2)]
    t = [x + _dot3(x, p) for x, p in zip(t, p4)]
    s = 3
    while (1 << s) < n:
        mask = same(s + 1) & jnp.logical_not(same(s))
        tl = [_dot3(x, jnp.where(mask, low, 0.0)) for x, low in zip(t, lows)]
        t = [x - _dot3(y, x) for x, y in zip(t, tl)]
        s += 1
    return t


def _modnorm(x, g, sc, sh):
    return _rms(x) * g * (1.0 + sc) + sh


def _ada_kernel(c_ref, w_ref, b_ref, o_ref):
    c = c_ref[...]
    o_ref[...] = _dot3(_silu(c), w_ref[...]) + b_ref[...]


def _ada_call(c_all, w_ada, b_ada):
    depth, d, n = w_ada.shape
    rows = c_all.shape[0]
    tn = 1024
    return pl.pallas_call(
        _ada_kernel,
        out_shape=jax.ShapeDtypeStruct((depth, rows, n), F32),
        grid=(depth, n // tn),
        in_specs=[
            pl.BlockSpec((rows, d), lambda l, j: (0, 0)),
            pl.BlockSpec((None, d, tn), lambda l, j: (l, 0, j)),
            pl.BlockSpec((None, 1, tn), lambda l, j: (l, 0, j)),
        ],
        out_specs=pl.BlockSpec((None, rows, tn), lambda l, j: (l, 0, j)),
        name="ada_mod",
    )(c_all, w_ada, b_ada.reshape(depth, 1, n))


_IN_GROUPS = ((0, 896), (896, 768), (1664, 256), (1920, 512), (2432, 512), (2944, 512), (3456, 128))
_IN_COLS = 3584


def _inproj_kernel(x_ref, g_ref, sc_ref, sh_ref, w_ref, *out_refs):
    hb = _modnorm(x_ref[...], g_ref[...], sc_ref[...], sh_ref[...]).astype(BF16)
    for (lo, wd), o_ref in zip(_IN_GROUPS, out_refs):
        o_ref[...] = _mm(hb, w_ref[:, lo:lo + wd])


def _inproj_prompt_kernel(x_ref, g_ref, sc_ref, sh_ref, w_ref, *refs, q_scale, aliased):
    pa_ref, qkv_ref, z_ref, ab_ref, qb_ref, kb_ref, vb_ref, kp_ref, vp_ref = refs[2:] if aliased else refs
    hb = _modnorm(x_ref[...], g_ref[...], sc_ref[...], sh_ref[...]).astype(BF16)
    g = [_mm(hb, w_ref[:, lo:lo + wd]) for lo, wd in _IN_GROUPS]
    pa_ref[...] = g[0]
    qkv_ref[...] = g[1]
    z_ref[...] = g[2]
    ab_ref[...] = g[6]
    qb_ref[...] = (g[3] * q_scale).astype(BF16)
    kb_ref[...] = g[4].astype(BF16)
    vb_ref[...] = g[5].astype(BF16)
    page = kp_ref.shape[-1]
    for p in range(kp_ref.shape[0]):
        kp_ref[p] = g[4][p * page:(p + 1) * page, :].T
        vp_ref[p] = g[5][p * page:(p + 1) * page, :].T


def _inproj_prompt_call(x, norm_g, l, sc, sh, w_in_b, tm, rows_per_seq, q_scale, page, depth, kp_buf, vp_buf):
    m, d = x.shape
    aliased = kp_buf is not None
    wq = _IN_GROUPS[3][1]
    row_outs = [((m, _IN_GROUPS[g][1]), F32) for g in (0, 1, 2, 6)] + [((m, wq), BF16)] * 3
    pages_shape = (depth, m // page, wq, page)
    page_spec = pl.BlockSpec((None, tm // page, wq, page), lambda i: (l, i, 0, 0))
    any_spec = pl.BlockSpec(memory_space=pl.ANY)
    return pl.pallas_call(
        functools.partial(_inproj_prompt_kernel, q_scale=q_scale, aliased=aliased),
        out_shape=tuple(jax.ShapeDtypeStruct(s, t) for s, t in row_outs)
        + (jax.ShapeDtypeStruct(pages_shape, F32),) * 2,
        grid=(m // tm,),
        in_specs=[
            pl.BlockSpec((tm, d), lambda i: (i, 0)),
            pl.BlockSpec((None, 1, d), lambda i: (4 * l, 0, 0)),
            _mod_spec(sc, tm, rows_per_seq),
            _mod_spec(sh, tm, rows_per_seq),
            pl.BlockSpec((None, d, _IN_COLS), lambda i: (l, 0, 0)),
        ] + ([any_spec, any_spec] if aliased else []),
        out_specs=tuple(pl.BlockSpec((tm, s[1]), lambda i: (i, 0)) for s, _ in row_outs) + (page_spec, page_spec),
        input_output_aliases={5: 7, 6: 8} if aliased else {},
        name="in_proj_prompt",
    )(x, norm_g, sc, sh, w_in_b, *((kp_buf, vp_buf) if aliased else ()))


def _mod_spec(arr, tm, rows_per_seq):
    d = arr.shape[-1]
    if arr.shape[1] == 1:
        return pl.BlockSpec((None, 1, d), lambda i: ((i * tm) // rows_per_seq, 0, 0))
    return pl.BlockSpec((None, tm, d), lambda i: (0, i, 0))


def _inproj_call(x, norm_g, l, slot, sc, sh, w_in_b, tm, rows_per_seq):
    m, d = x.shape
    outs = [((m, wd), F32) for _, wd in _IN_GROUPS]
    return pl.pallas_call(
        _inproj_kernel,
        out_shape=tuple(jax.ShapeDtypeStruct(s, t) for s, t in outs),
        grid=(m // tm,),
        in_specs=[
            pl.BlockSpec((tm, d), lambda i: (i, 0)),
            pl.BlockSpec((None, 1, d), lambda i: (4 * l + slot, 0, 0)),
            _mod_spec(sc, tm, rows_per_seq),
            _mod_spec(sh, tm, rows_per_seq),
            pl.BlockSpec((None, d, _IN_COLS), lambda i: (l, 0, 0)),
        ],
        out_specs=tuple(pl.BlockSpec((tm, s[1]), lambda i: (i, 0)) for s, _ in outs),
        name="in_proj",
    )(x, norm_g, sc, sh, w_in_b)


def _rwkv_mix(pa, prev, mu, vec, wlr):
    w = vec.shape[-1]
    xm = pa + mu * (prev - pa)
    r = xm[:, 0:w]
    k = xm[:, w:2 * w]
    v = xm[:, 2 * w:3 * w]
    t7 = xm[:, 3 * w:3 * w + LANES]
    w0, a0, k_k, k_a = vec[0:1], vec[1:2], vec[2:3], vec[3:4]
    r_k = vec[6:7]
    w_raw = w0 + _bdot(jnp.tanh(t7), wlr[0])
    lw = -jnp.exp(-_softplus(-w_raw) - 0.5)
    a = jax.nn.sigmoid(a0 + _bdot(t7, wlr[1]))
    gate = _bdot(jax.nn.sigmoid(t7), wlr[2])
    kkr = k * k_k
    kk = kkr * lax.rsqrt(_head_sum(kkr * kkr) + L2_EPS)
    k2 = k * (1.0 + (a - 1.0) * k_a)
    bonus = _head_sum(r * k2 * r_k) * v
    return r, lw, k2, v, kk, a, gate, bonus


def _rwkv_out(y, ln_w, ln_b, bonus, gate):
    mean = jnp.mean(y, axis=-1, keepdims=True)
    yc = y - mean
    var = jnp.mean(yc * yc, axis=-1, keepdims=True)
    return (yc * lax.rsqrt(var + RWKV_LN_EPS) * ln_w + ln_b + bonus) * gate


def _rwkv_chunk_kernel(pa_ref, shift_ref, s0_ref, mu_ref, vec_ref, wlr_ref, ya_ref, s_ref, carry_ref, *, heads):
    rows = pa_ref.shape[0]
    c = CHUNK
    nch = rows // c

    @pl.when(pl.program_id(1) == 0)
    def _():
        carry_ref[...] = shift_ref[...]
        s_ref[...] = s0_ref[...]

    pa = pa_ref[...]
    prev = jnp.where(_iota2(pa.shape, 0) == 0, carry_ref[...], pltpu.roll(pa, 1, 0))
    carry_ref[...] = pa[rows - 1:rows, :]
    vec = vec_ref[...]
    r, lw, k2, v, kk, a, gate, bonus = _rwkv_mix(pa, prev, mu_ref[...], vec, wlr_ref[...])

    trow = _iota2((rows, rows), 0)
    tcol = _iota2((rows, rows), 1)
    same_chunk = jnp.right_shift(trow, c.bit_length() - 1) == jnp.right_shift(tcol, c.bit_length() - 1)
    g = _dot_exact_lhs(_ones_where(same_chunk & (trow >= tcol)), lw)
    eg = jnp.exp(g)
    en = jnp.exp(-g)
    rt = r * eg
    kt = k2 * en
    at = kk * a * en
    bt = kk * jnp.exp(g - lw)
    row = _iota2((c, c), 0)
    col = _iota2((c, c), 1)
    incl = row >= col
    strict = row > col
    ones_c = jnp.ones((c, HEAD), BF16)
    hs = [slice(h * HEAD, (h + 1) * HEAD) for h in range(heads)]
    cs = [slice(i * c, (i + 1) * c) for i in range(nch)]
    ch = [(i, h) for i in range(nch) for h in range(heads)]
    lhs2 = {(i, h): jnp.concatenate([bt[cs[i], hs[h]], rt[cs[i], hs[h]]], axis=0) for i, h in ch}
    m4 = {(i, h): _bdot(lhs2[i, h], jnp.concatenate([kt[cs[i], hs[h]], at[cs[i], hs[h]]], axis=0), _NT)
          for i, h in ch}
    tinv = dict(zip(ch, _unit_lower_inverses([jnp.where(strict, m4[q][:c, c:], 0.0) for q in ch])))
    lkv = {(i, h): _bdot(jnp.where(strict, m4[i, h][:c, :c], 0.0), v[cs[i], hs[h]]) for i, h in ch}
    rkv = {(i, h): _bdot(jnp.where(incl, m4[i, h][c:, :c], 0.0), v[cs[i], hs[h]]) for i, h in ch}
    dec = {(i, h): jnp.exp(_dot_exact_rhs(lw[cs[i], hs[h]], ones_c, _TN)) for i, h in ch}
    egl = [eg[(i + 1) * c - 1:(i + 1) * c, :] for i in range(nch)]
    kgv = {(i, h): _bdot(kt[cs[i], hs[h]] * egl[i][:, hs[h]], v[cs[i], hs[h]], _TN) for i, h in ch}
    agt = {(i, h): at[cs[i], hs[h]] * egl[i][:, hs[h]] for i, h in ch}
    state = [s_ref[h] for h in range(heads)]
    out_rows = []
    for i in range(nch):
        proj = [_bdot(lhs2[i, h], state[h]) for h in range(heads)]
        u = [_dot3(tinv[i, h], proj[h][:c] + lkv[i, h]) for h in range(heads)]
        y = [proj[h][c:] + rkv[i, h] - _bdot(jnp.where(incl, m4[i, h][c:, c:], 0.0), u[h]) for h in range(heads)]
        state = [dec[i, h] * state[h] + kgv[i, h] - _bdot(agt[i, h], u[h], _TN) for h in range(heads)]
        out_rows.append(jnp.concatenate(
            [_rwkv_out(y[h], vec[4:5, hs[h]], vec[5:6, hs[h]], bonus[cs[i], hs[h]], gate[cs[i], hs[h]])
             for h in range(heads)], axis=1))
    s_ref[...] = jnp.stack(state, axis=0)
    ya_ref[...] = jnp.concatenate(out_rows, axis=0)


def _rwkv_prompt_call(pa, shift_prev, s0, mu, vec, wlr, l, batch, seq):
    m, cols = pa.shape
    heads = s0.shape[1]
    w = heads * HEAD
    rows = min(CHUNK * CHUNKS_PER_STEP, seq)
    nc = seq // rows
    return pl.pallas_call(
        functools.partial(_rwkv_chunk_kernel, heads=heads),
        out_shape=(jax.ShapeDtypeStruct((m, w), F32), jax.ShapeDtypeStruct(s0.shape, F32)),
        grid=(batch, nc),
        in_specs=[
            pl.BlockSpec((rows, cols), lambda b, i: (b * nc + i, 0)),
            pl.BlockSpec((None, 1, cols), lambda b, i: (b, 0, 0)),
            pl.BlockSpec((None, heads, HEAD, HEAD), lambda b, i: (b, 0, 0, 0)),
            pl.BlockSpec((None, 1, cols), lambda b, i: (l, 0, 0)),
            pl.BlockSpec((None, 8, w), lambda b, i: (l, 0, 0)),
            pl.BlockSpec((None, 3, LANES, w), lambda b, i: (l, 0, 0, 0)),
        ],
        out_specs=(
            pl.BlockSpec((rows, w), lambda b, i: (b * nc + i, 0)),
            pl.BlockSpec((None, heads, HEAD, HEAD), lambda b, i: (b, 0, 0, 0)),
        ),
        scratch_shapes=[pltpu.VMEM((1, cols), F32)],
        name="rwkv_chunk",
    )(pa, shift_prev, s0, mu, vec, wlr)


def _rwkv_prep_step_kernel(pa_ref, prev_ref, mu_ref, vec_ref, wlr_ref,
                           r_ref, w_ref, kk_ref, kka_ref, k_ref, v_ref, g_ref, b_ref):
    r, lw, k2, v, kk, a, gate, bonus = _rwkv_mix(pa_ref[...], prev_ref[...], mu_ref[...], vec_ref[...], wlr_ref[...])
    r_ref[...] = r.T
    w_ref[...] = jnp.exp(lw).T
    kk_ref[...] = kk.T
    kka_ref[...] = (kk * a).T
    k_ref[...] = k2.T
    v_ref[...] = v.T
    g_ref[...] = gate.T
    b_ref[...] = bonus.T


def _rwkv_prep_step_call(pa, prev, mu, vec, wlr, l):
    rows, cols = pa.shape
    w = vec.shape[-1]
    full = lambda shape: pl.BlockSpec(shape, lambda i: (0,) * len(shape))
    return pl.pallas_call(
        _rwkv_prep_step_kernel,
        out_shape=tuple(jax.ShapeDtypeStruct((w, rows), F32) for _ in range(8)),
        grid=(1,),
        in_specs=[
            full((rows, cols)), full((rows, cols)),
            pl.BlockSpec((None, 1, cols), lambda i: (l, 0, 0)),
            pl.BlockSpec((None, 8, w), lambda i: (l, 0, 0)),
            pl.BlockSpec((None, 3, LANES, w), lambda i: (l, 0, 0, 0)),
        ],
        out_specs=tuple(full((w, rows)) for _ in range(8)),
        name="rwkv_prep_step",
    )(pa, prev, mu, vec, wlr)


def _rwkv_state_step_kernel(s_ref, r_ref, w_ref, kk_ref, kka_ref, k_ref, v_ref, g_ref, b_ref, ln_ref,
                            so_ref, y_ref):
    nk = s_ref.shape[0]
    sa = kk_ref[0:1, :] * s_ref[0]
    for k in range(1, nk):
        sa = sa + kk_ref[k:k + 1, :] * s_ref[k]
    v = v_ref[...]
    y = None
    for k in range(nk):
        s_new = w_ref[k:k + 1, :] * s_ref[k] - kka_ref[k:k + 1, :] * sa + k_ref[k:k + 1, :] * v
        so_ref[k] = s_new
        t = r_ref[k:k + 1, :] * s_new
        y = t if y is None else y + t
    mean = jnp.mean(y, axis=0, keepdims=True)
    yc = y - mean
    var = jnp.mean(yc * yc, axis=0, keepdims=True)
    ln = ln_ref[...]
    y_ref[...] = (yc * lax.rsqrt(var + RWKV_LN_EPS) * ln[:, 0:1] + ln[:, 1:2] + b_ref[...]) * g_ref[...]


def _rwkv_state_step_call(s_t, vectors, ln, l):
    heads, rs = s_t.shape[1], s_t.shape[4]
    vec_spec = pl.BlockSpec((HEAD, rs), lambda h: (h, 0))
    return pl.pallas_call(
        _rwkv_state_step_kernel,
        out_shape=(jax.ShapeDtypeStruct((heads, HEAD, HEAD, rs), F32), jax.ShapeDtypeStruct((heads * HEAD, rs), F32)),
        grid=(heads,),
        in_specs=[pl.BlockSpec((None, None, HEAD, HEAD, rs), lambda h: (l, h, 0, 0, 0))] + [vec_spec] * 8
        + [pl.BlockSpec((None, HEAD, 2), lambda h: (l, h, 0))],
        out_specs=(pl.BlockSpec((None, HEAD, HEAD, rs), lambda h: (h, 0, 0, 0)), vec_spec),
        name="rwkv_state_step",
    )(s_t, *vectors, ln)


def _gdn_gates(ab, alog, dtb):
    glog = -jnp.exp(alog) * _softplus(ab + dtb)
    beta = jax.nn.sigmoid(ab)
    return glog, beta


def _gdn_qkv(conv_out, scale):
    w = conv_out.shape[-1] // 3
    act = _silu(conv_out)
    q = act[:, 0:w]
    k = act[:, w:2 * w]
    v = act[:, 2 * w:3 * w]
    qn = q * lax.rsqrt(_head_sum(q * q) + L2_EPS) * scale
    kn = k * lax.rsqrt(_head_sum(k * k) + L2_EPS)
    return qn, kn, v


def _gdn_chunk_kernel(x_ref, z_ref, ab_ref, conv0_ref, s0_ref, cw_ref, gp_ref, ng_ref,
                      yb_ref, s_ref, carry_ref, *, heads, scale):
    rows = x_ref.shape[0]
    c = CHUNK
    nch = rows // c

    @pl.when(pl.program_id(1) == 0)
    def _():
        carry_ref[...] = jnp.zeros(carry_ref.shape, F32)
        carry_ref[SUBLANES - 3:SUBLANES, :] = conv0_ref[...]
        s_ref[...] = s0_ref[...]

    x = x_ref[...]
    prev8 = carry_ref[...]
    cw = cw_ref[...]
    conv = x * cw[3:4]
    for d in (1, 2, 3):
        conv = conv + _shift_rows(x, prev8, d) * cw[3 - d:4 - d]
    carry_ref[...] = x[rows - SUBLANES:rows, :]
    qn, kn, v = _gdn_qkv(conv, scale)
    gp = gp_ref[...]
    glog, beta = _gdn_gates(ab_ref[...], gp[0:1], gp[1:2])

    trow = _iota2((rows, rows), 0)
    tcol = _iota2((rows, rows), 1)
    same_chunk = jnp.right_shift(trow, c.bit_length() - 1) == jnp.right_shift(tcol, c.bit_length() - 1)
    gc = _dot_exact_lhs(_ones_where(same_chunk & (trow >= tcol)), glog)
    gct = _dot_exact_rhs(glog, _ones_where(same_chunk & (trow <= tcol)), _TN)
    row = _iota2((c, c), 0)
    col = _iota2((c, c), 1)
    incl = row >= col
    strict = row > col
    ng = ng_ref[...]
    zg = _silu(z_ref[...])
    hs = [slice(h * HEAD, (h + 1) * HEAD) for h in range(heads)]
    cs = [slice(i * c, (i + 1) * c) for i in range(nch)]
    ch = [(i, h) for i in range(nch) for h in range(heads)]
    gcol = {(i, h): gc[cs[i], h:h + 1] for i, h in ch}
    bcol = {(i, h): beta[cs[i], heads + h:heads + h + 1] for i, h in ch}
    dec = {(i, h): jnp.where(incl, jnp.exp(jnp.where(incl, gcol[i, h] - gct[h:h + 1, cs[i]], 0.0)), 0.0)
           for i, h in ch}
    kb = {(i, h): kn[cs[i], hs[h]] * bcol[i, h] for i, h in ch}
    m2 = {(i, h): _bdot(jnp.concatenate([kb[i, h], qn[cs[i], hs[h]]], axis=0), kn[cs[i], hs[h]], _NT)
          for i, h in ch}
    tinv = dict(zip(ch, _unit_lower_inverses([jnp.where(strict, m2[q][:c] * dec[q], 0.0) for q in ch])))
    eg = {q: jnp.exp(gcol[q]) for q in ch}
    lhs = {(i, h): jnp.concatenate([kb[i, h] * eg[i, h], qn[cs[i], hs[h]] * eg[i, h]], axis=0) for i, h in ch}
    vb = {(i, h): v[cs[i], hs[h]] * bcol[i, h] for i, h in ch}
    qk = {q: m2[q][c:] * dec[q] for q in ch}
    glast = {q: gcol[q][c - 1:c, :] for q in ch}
    kd = {(i, h): kn[cs[i], hs[h]] * jnp.exp(glast[i, h] - gcol[i, h]) for i, h in ch}
    state = [s_ref[h] for h in range(heads)]
    out_rows = []
    for i in range(nch):
        ws = [_bdot(lhs[i, h], state[h]) for h in range(heads)]
        vn = [_dot3(tinv[i, h], vb[i, h] - ws[h][:c]) for h in range(heads)]
        o = [ws[h][c:] + _bdot(qk[i, h], vn[h]) for h in range(heads)]
        state = [state[h] * jnp.exp(glast[i, h]) + _bdot(kd[i, h], vn[h], _TN) for h in range(heads)]
        out_rows.append(jnp.concatenate([_rms(o[h]) * ng for h in range(heads)], axis=1))
    s_ref[...] = jnp.stack(state, axis=0)
    yb_ref[...] = jnp.concatenate(out_rows, axis=0) * zg


def _gdn_prompt_call(qkv, z, ab, conv0, s0, cw, gp, ng, l, batch, seq, scale):
    m, cols = qkv.shape
    heads = s0.shape[1]
    w = heads * HEAD
    rows = min(CHUNK * CHUNKS_PER_STEP, seq)
    nc = seq // rows
    return pl.pallas_call(
        functools.partial(_gdn_chunk_kernel, heads=heads, scale=scale),
        out_shape=(jax.ShapeDtypeStruct((m, w), F32), jax.ShapeDtypeStruct(s0.shape, F32)),
        grid=(batch, nc),
        in_specs=[
            pl.BlockSpec((rows, cols), lambda b, i: (b * nc + i, 0)),
            pl.BlockSpec((rows, w), lambda b, i: (b * nc + i, 0)),
            pl.BlockSpec((rows, LANES), lambda b, i: (b * nc + i, 0)),
            pl.BlockSpec((None, 3, cols), lambda b, i: (b, 0, 0)),
            pl.BlockSpec((None, heads, HEAD, HEAD), lambda b, i: (b, 0, 0, 0)),
            pl.BlockSpec((None, 4, cols), lambda b, i: (l, 0, 0)),
            pl.BlockSpec((None, 2, LANES), lambda b, i: (l, 0, 0)),
            pl.BlockSpec((None, 1, HEAD), lambda b, i: (l, 0, 0)),
        ],
        out_specs=(
            pl.BlockSpec((rows, w), lambda b, i: (b * nc + i, 0)),
            pl.BlockSpec((None, heads, HEAD, HEAD), lambda b, i: (b, 0, 0, 0)),
        ),
        scratch_shapes=[pltpu.VMEM((SUBLANES, cols), F32)],
        name="gdn_chunk",
    )(qkv, z, ab, conv0, s0, cw, gp, ng)


def _gdn_prep_step_kernel(x_ref, buf_ref, z_ref, ab_ref, cw_ref, gp_ref,
                          q_ref, k_ref, v_ref, dec_ref, beta_ref, zs_ref, *, scale):
    cw = cw_ref[...]
    conv = x_ref[...] * cw[3:4]
    for d in range(3):
        conv = conv + buf_ref[d] * cw[d:d + 1]
    qn, kn, v = _gdn_qkv(conv, scale)
    gp = gp_ref[...]
    glog, beta = _gdn_gates(ab_ref[...], gp[0:1], gp[1:2])
    q_ref[...] = qn.T
    k_ref[...] = kn.T
    v_ref[...] = v.T
    dec_ref[...] = jnp.exp(glog).T
    beta_ref[...] = beta.T
    zs_ref[...] = _silu(z_ref[...]).T


def _gdn_prep_step_call(x, bufs, z, ab, cw, gp, l, heads, scale):
    rows, cols = x.shape
    w = heads * HEAD
    full = lambda shape: pl.BlockSpec(shape, lambda i: (0,) * len(shape))
    out_shapes = [(w, rows), (w, rows), (w, rows), (LANES, rows), (LANES, rows), (w, rows)]
    return pl.pallas_call(
        functools.partial(_gdn_prep_step_kernel, scale=scale),
        out_shape=tuple(jax.ShapeDtypeStruct(s, F32) for s in out_shapes),
        grid=(1,),
        in_specs=[
            full((rows, cols)), full((3, rows, cols)), full((rows, w)), full((rows, LANES)),
            pl.BlockSpec((None, 4, cols), lambda i: (l, 0, 0)),
            pl.BlockSpec((None, 2, LANES), lambda i: (l, 0, 0)),
        ],
        out_specs=tuple(full(s) for s in out_shapes),
        name="gdn_prep_step",
    )(x, bufs, z, ab, cw, gp)


def _gdn_state_step_kernel(s_ref, q_ref, k_ref, v_ref, dec_ref, beta_ref, zs_ref, ng_ref, so_ref, y_ref):
    nk = s_ref.shape[0]
    dec = dec_ref[...]
    kv = k_ref[0:1, :] * (s_ref[0] * dec)
    for k in range(1, nk):
        kv = kv + k_ref[k:k + 1, :] * (s_ref[k] * dec)
    vn = (v_ref[...] - kv) * beta_ref[...]
    o = None
    for k in range(nk):
        s_new = s_ref[k] * dec + k_ref[k:k + 1, :] * vn
        so_ref[k] = s_new
        t = q_ref[k:k + 1, :] * s_new
        o = t if o is None else o + t
    y_ref[...] = o * lax.rsqrt(jnp.mean(o * o, axis=0, keepdims=True) + NORM_EPS) * ng_ref[...] * zs_ref[...]


def _gdn_state_step_call(s_t, q_t, k_t, v_t, dec_t, beta_t, zs_t, ng_col, l):
    heads, rs = s_t.shape[1], s_t.shape[4]
    vec_spec = pl.BlockSpec((HEAD, rs), lambda h: (h, 0))
    gate_spec = pl.BlockSpec((None, 1, rs), lambda h: (h, 0, 0))
    return pl.pallas_call(
        _gdn_state_step_kernel,
        out_shape=(jax.ShapeDtypeStruct((heads, HEAD, HEAD, rs), F32), jax.ShapeDtypeStruct((heads * HEAD, rs), F32)),
        grid=(heads,),
        in_specs=[pl.BlockSpec((None, None, HEAD, HEAD, rs), lambda h: (l, h, 0, 0, 0)),
                  vec_spec, vec_spec, vec_spec, gate_spec, gate_spec, vec_spec,
                  pl.BlockSpec((None, HEAD, 1), lambda h: (l, 0, 0))],
        out_specs=(pl.BlockSpec((None, HEAD, HEAD, rs), lambda h: (h, 0, 0, 0)), vec_spec),
        name="gdn_state_step",
    )(s_t, q_t, k_t, v_t, dec_t, beta_t, zs_t, ng_col)


def _log_sigmoid_pair(z):
    lsn = -(jnp.maximum(z, 0.0) + jnp.log(1.0 + jnp.exp(-jnp.abs(z))))
    return z + lsn, lsn


def _suffix_matrix(n, copies=1):
    assert n & (n - 1) == 0
    r = _iota2((copies * n, n + LANES), 0) & (n - 1)
    c = _iota2((copies * n, n + LANES), 1)
    return _ones_where((r > c) | (c >= n))


def _sb_tile(q_ref, k_ref, v_ref, bias_ref, acc_ref, carry_ref, off, *, heads, layer, masked):
    bq, bk = q_ref.shape[0], k_ref.shape[0]
    cmat = _suffix_matrix(bk, copies=2)
    if masked:
        reach = _iota2((bq, bk), 1) < _iota2((bq, bk), 0) + off
    hs = [slice(h * HEAD, (h + 1) * HEAD) for h in range(heads)]
    carries = [carry_ref[h] for h in range(heads)]
    pv, new_carries = [], []
    for h0 in range(0, heads, SB_HEAD_GROUP):
        grp = range(h0, min(h0 + SB_HEAD_GROUP, heads))
        zs = [_mm(q_ref[:, hs[h]], k_ref[:, hs[h]], _NT) + bias_ref[layer, h] for h in grp]
        ls = [_log_sigmoid_pair(z) for z in zs]
        lsn = [jnp.where(reach, p[1], 0.0) if masked else p[1] for p in ls]
        cs = [_mm(jnp.concatenate(_split2(x), axis=1), cmat) for x in lsn]
        wgt = [jnp.exp(p[0] + c[:, :bk] + carries[h]) for p, c, h in zip(ls, cs, grp)]
        if masked:
            wgt = [jnp.where(reach, w, 0.0) for w in wgt]
        pv += [_mm(w.astype(BF16), v_ref[:, hs[h]]) for w, h in zip(wgt, grp)]
        new_carries += [carries[h] + c[:, bk:] for c, h in zip(cs, grp)]
    acc_ref[...] += jnp.concatenate(pv, axis=1)
    for h in range(heads):
        carry_ref[h] = new_carries[h]


def _sb_prompt_kernel(qi_ref, kj_ref, fl_ref, bias_ref, q_ref, k_ref, v_ref, ng_ref, o_ref, acc_ref, carry_ref,
                      *, heads, layer):
    p = pl.program_id(1)
    kj = kj_ref[p]
    flags = fl_ref[p]
    bq, bk = q_ref.shape[0], k_ref.shape[0]
    off = qi_ref[p] * bq - kj * bk
    tile = functools.partial(_sb_tile, q_ref, k_ref, v_ref, bias_ref, acc_ref, carry_ref, off,
                             heads=heads, layer=layer)

    @pl.when((flags & 1) != 0)
    def _():
        acc_ref[...] = jnp.zeros(acc_ref.shape, F32)
        carry_ref[...] = jnp.zeros(carry_ref.shape, F32)

    @pl.when((flags & 2) != 0)
    def _():
        tile(masked=True)

    @pl.when((flags & 2) == 0)
    def _():
        tile(masked=False)

    @pl.when(kj == 0)
    def _():
        ng = ng_ref[...]
        o_ref[...] = jnp.concatenate(
            [_rms(acc_ref[:, h * HEAD:(h + 1) * HEAD]) * ng for h in range(heads)], axis=1)


def _sb_prompt_call(q, k, v, bias, ng, l, batch, seq):
    m, w = q.shape
    heads = w // HEAD
    bq = min(SB_Q_BLOCK, seq)
    bk = SB_K_BLOCK
    nq, nk = seq // bq, seq // bk
    qi_l, kj_l, fl_l = [], [], []
    for qi in range(nq):
        last = ((qi + 1) * bq - 2) // bk
        for kj in range(last, -1, -1):
            qi_l.append(qi)
            kj_l.append(kj)
            fl_l.append((1 if kj == last else 0) | (2 if (kj + 1) * bk > qi * bq else 0))
    tables = [jnp.asarray(np.array(t, np.int32)) for t in (qi_l, kj_l, fl_l)]
    grid_spec = pltpu.PrefetchScalarGridSpec(
        num_scalar_prefetch=3,
        grid=(batch, len(qi_l)),
        in_specs=[
            pl.BlockSpec(memory_space=pltpu.SMEM),
            pl.BlockSpec((bq, w), lambda b, p, qt, kt, ft: (b * nq + qt[p], 0)),
            pl.BlockSpec((bk, w), lambda b, p, qt, kt, ft: (b * nk + kt[p], 0)),
            pl.BlockSpec((bk, w), lambda b, p, qt, kt, ft: (b * nk + kt[p], 0)),
            pl.BlockSpec((None, 1, HEAD), lambda b, p, qt, kt, ft: (l, 0, 0)),
        ],
        out_specs=pl.BlockSpec((bq, w), lambda b, p, qt, kt, ft: (b * nq + qt[p], 0)),
        scratch_shapes=[pltpu.VMEM((bq, w), F32), pltpu.VMEM((heads, bq, LANES), F32)],
    )
    return pl.pallas_call(
        functools.partial(_sb_prompt_kernel, heads=heads, layer=l),
        out_shape=jax.ShapeDtypeStruct((m, w), F32),
        grid_spec=grid_spec,
        name="sb_prompt",
    )(*tables, bias, q, k, v, ng)


def _sb_decode_kernel(pt_ref, q_ref, *refs, heads, scale, group):
    k_refs, v_refs = refs[:group], refs[group:2 * group]
    bias_ref, ng_ref, o_ref, acc_ref, carry_ref = refs[2 * group:]
    step = pl.program_id(1)
    ps = k_refs[0].shape[2]

    @pl.when(step == 0)
    def _():
        acc_ref[...] = jnp.zeros(acc_ref.shape, F32)
        carry_ref[...] = jnp.zeros(carry_ref.shape, F32)

    qb = (q_ref[...] * scale).astype(BF16)
    head_z = _iota2((heads, ps), 0)
    head_y = _iota2((heads, HEAD), 0)
    cmat = _suffix_matrix(ps)
    bias = bias_ref[...]
    zs = []
    for g in range(group):
        z = jnp.zeros((heads, ps), F32)
        for h in range(heads):
            z = jnp.where(head_z == h, _mm(qb, k_refs[g][h].astype(BF16)), z)
        zs.append(z + bias)
    ls = [_log_sigmoid_pair(z) for z in zs]
    cs = [_dot_exact_rhs(p[1], cmat) for p in ls]
    carry = carry_ref[...]
    wgt = []
    for g in range(group):
        wgt.append(jnp.exp(ls[g][0] + cs[g][:, :ps] + carry).astype(BF16))
        carry = carry + cs[g][:, ps:]
    ys = [[_mm(wgt[g], v_refs[g][h].astype(BF16), _NT) for h in range(heads)] for g in range(group)]
    acc = acc_ref[...]
    for h in range(heads):
        yh = ys[0][h]
        for g in range(1, group):
            yh = yh + ys[g][h]
        acc = jnp.where(head_y == h, acc + yh, acc)
    acc_ref[...] = acc
    carry_ref[...] = carry

    @pl.when(step == pl.num_programs(1) - 1)
    def _():
        o_ref[...] = _rms(acc) * ng_ref[...]


def _sb_decode_call(q, cache_k, cache_v, page_table, bias_b, ng, l, scale):
    rows, w = q.shape
    heads = w // HEAD
    npages = page_table.shape[1]
    depth, npool, ps = cache_k.shape[0], cache_k.shape[1], cache_k.shape[2]
    group = DEC_PAGES if npages % DEC_PAGES == 0 else 1
    ck = jnp.transpose(cache_k, (0, 1, 3, 4, 2))
    cv = jnp.transpose(cache_v, (0, 1, 3, 4, 2))

    def page_spec(g):
        return pl.BlockSpec((None, None, heads, HEAD, ps),
                            lambda b, s, pt: (l, pt[b * npages + npages - 1 - (s * group + g)], 0, 0, 0))

    grid_spec = pltpu.PrefetchScalarGridSpec(
        num_scalar_prefetch=1,
        grid=(rows, npages // group),
        in_specs=[pl.BlockSpec((None, heads, HEAD), lambda b, s, pt: (b, 0, 0))]
        + [page_spec(g) for g in range(group)] * 2
        + [pl.BlockSpec((None, heads, LANES), lambda b, s, pt: (l, 0, 0)),
           pl.BlockSpec((None, 1, HEAD), lambda b, s, pt: (l, 0, 0))],
        out_specs=pl.BlockSpec((None, heads, HEAD), lambda b, s, pt: (b, 0, 0)),
        scratch_shapes=[pltpu.VMEM((heads, HEAD), F32), pltpu.VMEM((heads, LANES), F32)],
    )
    out = pl.pallas_call(
        functools.partial(_sb_decode_kernel, heads=heads, scale=scale, group=group),
        out_shape=jax.ShapeDtypeStruct((rows, heads, HEAD), F32),
        grid_spec=grid_spec,
        name="sb_decode",
    )(page_table.reshape(-1), q.reshape(rows, heads, HEAD), *([ck] * group), *([cv] * group), bias_b, ng)
    return out.reshape(rows, w)


def _outproj_kernel(ya_ref, yb_ref, yc_ref, x_ref, gt_ref, g_ref, w_ref, o_ref, *, channel_major):
    dims = _TN if channel_major else _NN
    wa, wb = ya_ref.shape[dims[0][0][0]], yb_ref.shape[dims[0][0][0]]
    y = (_bdot(ya_ref[...], w_ref[0:wa, :], dims) + _bdot(yb_ref[...], w_ref[wa:wa + wb, :], dims)
         + _bdot(yc_ref[...], w_ref[wa + wb:, :]))
    o_ref[...] = x_ref[...] + gt_ref[...] * (_rms(y) * g_ref[...])


def _outproj_call(ya, yb, yc, x, gt, norm_g, l, w_out_b, tm, rows_per_seq, channel_major=False):
    m, d = x.shape
    row_spec = lambda a: pl.BlockSpec((tm, a.shape[1]), lambda i: (i, 0))
    mix_spec = (lambda a: pl.BlockSpec((a.shape[0], tm), lambda i: (0, i))) if channel_major else row_spec
    return pl.pallas_call(
        functools.partial(_outproj_kernel, channel_major=channel_major),
        out_shape=jax.ShapeDtypeStruct((m, d), F32),
        grid=(m // tm,),
        in_specs=[
            mix_spec(ya), mix_spec(yb), row_spec(yc), row_spec(x),
            _mod_spec(gt, tm, rows_per_seq),
            pl.BlockSpec((None, 1, d), lambda i: (4 * l + 1, 0, 0)),
            pl.BlockSpec((None, w_out_b.shape[1], d), lambda i: (l, 0, 0)),
        ],
        out_specs=pl.BlockSpec((tm, d), lambda i: (i, 0)),
        name="out_proj",
    )(ya, yb, yc, x, gt, norm_g, w_out_b)


def _ffn_up_kernel(x_ref, g_ref, sc_ref, sh_ref, wu_ref, wg_ref, cw_ref, st_ref,
                   act_ref, tail_ref, h_ref, carry_ref, *, rows_per_seq):
    i = pl.program_id(0)
    j = pl.program_id(1)
    tm = x_ref.shape[0]

    @pl.when(j == 0)
    def _():
        h_ref[...] = _modnorm(x_ref[...], g_ref[...], sc_ref[...], sh_ref[...]).astype(BF16)

    @pl.when((i * tm) % rows_per_seq == 0)
    def _():
        carry_ref[j] = jnp.zeros(carry_ref.shape[1:], F32)
        carry_ref[j, SUBLANES - 2:SUBLANES, :] = st_ref[...]

    hb = h_ref[...]
    u = _mm(hb, wu_ref[...])
    gv = _mm(hb, wg_ref[...])
    prev8 = carry_ref[j]
    cw = cw_ref[...]
    uc = _shift_rows(u, prev8, 2) * cw[0:1] + _shift_rows(u, prev8, 1) * cw[1:2] + u * cw[2:3]
    tail = u[tm - SUBLANES:tm, :]
    carry_ref[j] = tail
    tail_ref[...] = tail
    act_ref[...] = _gelu_tanh(uc) * gv


def _ffn_up_call(x, norm_g, l, sc, sh, w_gu_b, cw, conv0, tm, batch, seq):
    m, d = x.shape
    dff = cw.shape[-1]
    tn = dff // 2
    nn = dff // tn
    return pl.pallas_call(
        functools.partial(_ffn_up_kernel, rows_per_seq=seq),
        out_shape=(jax.ShapeDtypeStruct((m, dff), F32), jax.ShapeDtypeStruct((m // tm, SUBLANES, dff), F32)),
        grid=(m // tm, nn),
        in_specs=[
            pl.BlockSpec((tm, d), lambda i, j: (i, 0)),
            pl.BlockSpec((None, 1, d), lambda i, j: (4 * l + 2, 0, 0)),
            pl.BlockSpec((None, 1, d), lambda i, j: ((i * tm) // seq, 0, 0)),
            pl.BlockSpec((None, 1, d), lambda i, j: ((i * tm) // seq, 0, 0)),
            pl.BlockSpec((None, d, tn), lambda i, j: (l, 0, j)),
            pl.BlockSpec((None, d, tn), lambda i, j: (l, 0, nn + j)),
            pl.BlockSpec((None, 3, tn), lambda i, j: (l, 0, j)),
            pl.BlockSpec((None, 2, tn), lambda i, j: ((i * tm) // seq, 0, j)),
        ],
        out_specs=(
            pl.BlockSpec((tm, tn), lambda i, j: (i, j)),
            pl.BlockSpec((None, SUBLANES, tn), lambda i, j: (i, 0, j)),
        ),
        scratch_shapes=[pltpu.VMEM((tm, d), BF16), pltpu.VMEM((nn, SUBLANES, tn), F32)],
        name="ffn_up",
    )(x, norm_g, sc, sh, w_gu_b, w_gu_b, cw, conv0)


def _ffn_up_step_kernel(x_ref, g_ref, sc_ref, sh_ref, wu_ref, wg_ref, cw_ref, buf_ref, act_ref, u_ref, h_ref):
    @pl.when(pl.program_id(0) == 0)
    def _():
        h_ref[...] = _modnorm(x_ref[...], g_ref[...], sc_ref[...], sh_ref[...]).astype(BF16)

    hb = h_ref[...]
    u = _mm(hb, wu_ref[...])
    gv = _mm(hb, wg_ref[...])
    cw = cw_ref[...]
    uc = buf_ref[0] * cw[0:1] + buf_ref[1] * cw[1:2] + u * cw[2:3]
    u_ref[...] = u
    act_ref[...] = _gelu_tanh(uc) * gv


def _ffn_up_step_call(x, norm_g, l, sc, sh, w_gu_b, cw, bufs):
    rows, d = x.shape
    dff = cw.shape[-1]
    tn = dff // 2
    nn = dff // tn
    return pl.pallas_call(
        _ffn_up_step_kernel,
        out_shape=(jax.ShapeDtypeStruct((rows, dff), F32), jax.ShapeDtypeStruct((rows, dff), F32)),
        grid=(nn,),
        in_specs=[
            pl.BlockSpec((rows, d), lambda j: (0, 0)),
            pl.BlockSpec((None, 1, d), lambda j: (4 * l + 2, 0, 0)),
            pl.BlockSpec((None, rows, d), lambda j: (0, 0, 0)),
            pl.BlockSpec((None, rows, d), lambda j: (0, 0, 0)),
            pl.BlockSpec((None, d, tn), lambda j: (l, 0, j)),
            pl.BlockSpec((None, d, tn), lambda j: (l, 0, nn + j)),
            pl.BlockSpec((None, 3, tn), lambda j: (l, 0, j)),
            pl.BlockSpec((2, rows, tn), lambda j: (0, 0, j)),
        ],
        out_specs=(pl.BlockSpec((rows, tn), lambda j: (0, j)), pl.BlockSpec((rows, tn), lambda j: (0, j))),
        scratch_shapes=[pltpu.VMEM((rows, d), BF16)],
        name="ffn_up_step",
    )(x, norm_g, sc, sh, w_gu_b, w_gu_b, cw, bufs)


def _ffn_down_kernel(a_ref, x_ref, gt_ref, g_ref, w_ref, o_ref):
    y = _bdot(a_ref[...], w_ref[...])
    o_ref[...] = x_ref[...] + gt_ref[...] * (_rms(y) * g_ref[...])


def _ffn_down_call(act, x, gt, norm_g, l, w_dn_b, tm, rows_per_seq):
    m, d = x.shape
    dff = act.shape[1]
    return pl.pallas_call(
        _ffn_down_kernel,
        out_shape=jax.ShapeDtypeStruct((m, d), F32),
        grid=(m // tm,),
        in_specs=[
            pl.BlockSpec((tm, dff), lambda i: (i, 0)),
            pl.BlockSpec((tm, d), lambda i: (i, 0)),
            _mod_spec(gt, tm, rows_per_seq),
            pl.BlockSpec((None, 1, d), lambda i: (4 * l + 3, 0, 0)),
            pl.BlockSpec((None, dff, d), lambda i: (l, 0, 0)),
        ],
        out_specs=pl.BlockSpec((tm, d), lambda i: (i, 0)),
        name="ffn_down",
    )(act, x, gt, norm_g, w_dn_b)


def kernel(x_prompt, x_sample, state_rwkv, state_rwkv_shift, state_gdn, state_gdn_conv, cache_k, cache_v, state_ffn_conv, page_table, c_prompt, c_sample, w_ada, b_ada, norm_g, w_in, rwkv_mu, rwkv_w0, rwkv_w_up, rwkv_a0, rwkv_a_up, rwkv_g_up, rwkv_k_k, rwkv_k_a, rwkv_r_k, rwkv_ln_w, rwkv_ln_b, gdn_conv_w, gdn_a_log, gdn_dt_bias, gdn_norm_g, sb_norm_g, sb_bias, w_out, w_gate_up, ffn_conv_w, w_down):
    bp, seq, d = x_prompt.shape
    rs = x_sample.shape[0]
    depth = w_ada.shape[0]
    h_a, h_b, h_c = state_rwkv.shape[2], state_gdn.shape[2], cache_k.shape[3]
    w_a, w_b, w_c = h_a * HEAD, h_b * HEAD, h_c * HEAD
    page = cache_k.shape[2]
    dff = w_down.shape[1]
    scale = HEAD ** -0.5
    rk_w, ri_w, rg_w = rwkv_w_up.shape[1], rwkv_a_up.shape[1], rwkv_g_up.shape[1]
    rwkv_cols = 3 * w_a + rk_w + ri_w + rg_w
    gdn_qkv = 3 * w_b
    gdn_cols = gdn_qkv + w_b + 2 * h_b
    assert rk_w + ri_w + rg_w == LANES and w_a == w_b and 2 * h_b <= LANES
    assert seq % CHUNK == 0 and rs % LANES == 0 and seq % page == 0
    assert (rwkv_cols, gdn_qkv, w_b, w_c, w_c, w_c) == tuple(wd for _, wd in _IN_GROUPS[:6])

    g_end = rwkv_cols + gdn_qkv + w_b
    ab_end = rwkv_cols + gdn_cols
    w_in_b = jnp.concatenate(
        [w_in[:, :, :g_end], w_in[:, :, ab_end:], w_in[:, :, g_end:ab_end],
         jnp.zeros((depth, d, LANES - 2 * h_b), F32)], axis=-1).astype(BF16)
    w_out_b = w_out.astype(BF16)
    w_gu_b = w_gate_up.astype(BF16)
    w_dn_b = w_down.astype(BF16)
    norm_g3 = norm_g.reshape(depth * 4, 1, d)
    wlr = jnp.stack([
        jnp.pad(rwkv_w_up, ((0, 0), (0, LANES - rk_w), (0, 0))),
        jnp.pad(rwkv_a_up, ((0, 0), (rk_w, LANES - rk_w - ri_w), (0, 0))),
        jnp.pad(rwkv_g_up, ((0, 0), (rk_w + ri_w, 0), (0, 0))),
    ], axis=1)
    vec = jnp.stack([rwkv_w0, rwkv_a0, rwkv_k_k, rwkv_k_a, rwkv_ln_w, rwkv_ln_b,
                     rwkv_r_k.reshape(depth, w_a), jnp.zeros((depth, w_a), F32)], axis=1)
    mu3 = rwkv_mu.reshape(depth, 1, rwkv_cols)
    gp = jnp.pad(jnp.stack([gdn_a_log, gdn_dt_bias], axis=1), ((0, 0), (0, 0), (0, LANES - h_b)))
    gdn_ng = gdn_norm_g.reshape(depth, 1, HEAD)
    gdn_ng_col = gdn_norm_g.reshape(depth, HEAD, 1)
    rwkv_ln = jnp.stack([rwkv_ln_w, rwkv_ln_b], axis=-1)
    rwkv_t = jnp.transpose(state_rwkv, (0, 2, 3, 4, 1))
    gdn_t = jnp.transpose(state_gdn, (0, 2, 3, 4, 1))
    sb_ng = sb_norm_g.reshape(depth, 1, HEAD)
    bias_b = jnp.broadcast_to(sb_bias[:, :, None], (depth, h_c, LANES))

    rows_c = bp + rs
    pad_c = (-rows_c) % SUBLANES
    c_all = jnp.concatenate([c_prompt, c_sample, jnp.zeros((pad_c, d), F32)], axis=0)
    mod = _ada_call(c_all, w_ada, b_ada)

    xp = x_prompt.reshape(bp * seq, d)
    xs = x_sample.reshape(rs, d)
    tm_p = min(ROW_TILE, seq)
    assert seq % tm_p == 0
    zeros_shift = jnp.zeros((bp, 1, rwkv_cols), F32)
    zeros_state_a = jnp.zeros((bp, h_a, HEAD, HEAD), F32)
    zeros_state_b = jnp.zeros((bp, h_b, HEAD, HEAD), F32)
    zeros_gconv = jnp.zeros((bp, 3, gdn_qkv), F32)
    zeros_fconv = jnp.zeros((bp, 2, dff), F32)

    outs_p = [[] for _ in range(5)]
    outs_s = [[] for _ in range(7)]
    kp_buf = vp_buf = None
    for l in range(depth):
        mp = [mod[l, :bp, i * d:(i + 1) * d].reshape(bp, 1, d) for i in range(6)]
        ms = [mod[l, bp:bp + rs, i * d:(i + 1) * d].reshape(1, rs, d) for i in range(6)]

        pa, qkv, z, ab, q_b, k_b, v_b, kp_buf, vp_buf = _inproj_prompt_call(
            xp, norm_g3, l, mp[1], mp[0], w_in_b, tm_p, seq, scale, page, depth, kp_buf, vp_buf)
        ya, rwkv_p = _rwkv_prompt_call(pa, zeros_shift, zeros_state_a, mu3, vec, wlr, l, bp, seq)
        yb, gdn_p = _gdn_prompt_call(qkv, z, ab, zeros_gconv, zeros_state_b, gdn_conv_w, gp, gdn_ng, l, bp, seq, scale)
        yc = _sb_prompt_call(q_b, k_b, v_b, sb_bias, sb_ng, l, bp, seq)
        xp = _outproj_call(ya, yb, yc, xp, mp[2], norm_g3, l, w_out_b, tm_p, seq)
        act, tail = _ffn_up_call(xp, norm_g3, l, mp[4], mp[3], w_gu_b, ffn_conv_w, zeros_fconv, tm_p, bp, seq)
        xp = _ffn_down_call(act, xp, mp[5], norm_g3, l, w_dn_b, tm_p, seq)
        outs_p[0].append(pa.reshape(bp, seq, rwkv_cols)[:, -1])
        outs_p[1].append(rwkv_p)
        outs_p[2].append(gdn_p)
        outs_p[3].append(qkv.reshape(bp, seq, gdn_qkv)[:, -3:])
        outs_p[4].append(tail.reshape(bp, seq // tm_p, SUBLANES, dff)[:, -1, SUBLANES - 2:])

        pa, qkv, z, sq, sk, sv, ab = _inproj_call(xs, norm_g3, l, 0, ms[1], ms[0], w_in_b, rs, 1)
        rwkv_vecs = _rwkv_prep_step_call(pa, state_rwkv_shift[l], mu3, vec, wlr, l)
        rwkv_s, ya = _rwkv_state_step_call(rwkv_t, rwkv_vecs, rwkv_ln, l)
        gbuf = jnp.swapaxes(state_gdn_conv[l], 0, 1)
        qn, kn, gv_, dec, beta, zs = _gdn_prep_step_call(qkv, gbuf, z, ab, gdn_conv_w, gp, l, h_b, scale)
        gdn_s, yb = _gdn_state_step_call(
            gdn_t, qn, kn, gv_, dec[:h_b].reshape(h_b, 1, rs), beta[h_b:2 * h_b].reshape(h_b, 1, rs), zs,
            gdn_ng_col, l)
        yc = _sb_decode_call(sq, cache_k, cache_v, page_table, bias_b, sb_ng, l, scale)
        xs = _outproj_call(ya, yb, yc, xs, ms[2], norm_g3, l, w_out_b, rs, 1, channel_major=True)
        fbuf = jnp.swapaxes(state_ffn_conv[l], 0, 1)
        act, u = _ffn_up_step_call(xs, norm_g3, l, ms[4], ms[3], w_gu_b, ffn_conv_w, fbuf)
        xs = _ffn_down_call(act, xs, ms[5], norm_g3, l, w_dn_b, rs, 1)
        outs_s[0].append(pa)
        outs_s[1].append(rwkv_s)
        outs_s[2].append(gdn_s)
        outs_s[3].append(jnp.concatenate([state_gdn_conv[l][:, 1:], qkv[:, None, :]], axis=1))
        outs_s[4].append(jnp.concatenate([state_ffn_conv[l][:, 1:], u[:, None, :]], axis=1))
        outs_s[5].append(sk.reshape(rs, 1, h_c, HEAD))
        outs_s[6].append(sv.reshape(rs, 1, h_c, HEAD))

    shift_p, rwkv_p, gdn_p, gconv_p, fconv_p = [jnp.stack(o, axis=0) for o in outs_p]
    pages_out = lambda buf: jnp.transpose(
        buf.reshape(depth, bp, seq // page, h_c, HEAD, page), (0, 1, 2, 5, 3, 4))
    k_p, v_p = pages_out(kp_buf), pages_out(vp_buf)
    shift_s, rwkv_s, gdn_s, gconv_s, fconv_s, k_s, v_s = [jnp.stack(o, axis=0) for o in outs_s]
    rwkv_s = jnp.transpose(rwkv_s, (0, 4, 1, 2, 3))
    gdn_s = jnp.transpose(gdn_s, (0, 4, 1, 2, 3))
    return (xp.reshape(bp, seq, d), xs.reshape(rs, 1, d), rwkv_p, rwkv_s, shift_p, shift_s, gdn_p, gdn_s,
            gconv_p, gconv_s, k_p, k_s, v_p, v_s, fconv_p, fconv_s)
```
